```python
import math
import jax
import jax.numpy as jnp
from jax import lax
import numpy as np

D_MODEL = 1024
BATCH = 8
SEQ = 4096
DEPTH = 2

GRID_W = 64
CTX_LEN = 256
HEAD_DIM = 64
NA_HEADS = D_MODEL // 4 // HEAD_DIM
NA_WIN_ROWS = 8
NA_WIN_COLS = 16
DA_QK_DIM = 32
DA_V_DIM = 2 * DA_QK_DIM
DA_HEADS = D_MODEL // 4 // DA_V_DIM
GLA_DV = 128
GLA_DK = GLA_DV // 2
GLA_HEADS = D_MODEL // 2 // GLA_DV
GLA_GATE_RANK = 16
GLA_GATE_NORM = 16.0
GLA_CHUNK = 64
FFN_DIM = ((8 * D_MODEL) // 3 + 127) // 128 * 128
CONV_W = 3
Q_BLOCK = 128
ROPE_THETA = 10000.0
EPS = 1e-6
NEG_INF = -1e30
F32 = jnp.float32

NA_W = NA_HEADS * HEAD_DIM
DA_QK_W = DA_HEADS * 2 * DA_QK_DIM
DA_V_W = DA_HEADS * DA_V_DIM
GLA_K_W = GLA_HEADS * GLA_DK
GLA_V_W = GLA_HEADS * GLA_DV
IN_SIZES = (('qa', NA_W), ('ka', NA_W), ('va', NA_W),
            ('qb', DA_QK_W), ('kb', DA_QK_W), ('vb', DA_V_W),
            ('qc', GLA_K_W), ('kc', GLA_K_W), ('vc', GLA_V_W), ('gc', GLA_V_W),
            ('af', GLA_GATE_RANK), ('ab', GLA_GATE_RANK))
IN_W = 3 * NA_W + 2 * DA_QK_W + DA_V_W + 2 * GLA_K_W + 2 * GLA_V_W + 2 * GLA_GATE_RANK
MIX_W = NA_W + DA_V_W + GLA_V_W

kernel_name = 'hybrid_na_diff_gla_prefix_dit'


def rms_norm(x, g):
    xf = x.astype(F32)
    y = xf * lax.rsqrt(jnp.mean(xf * xf, axis=-1, keepdims=True) + EPS)
    return (y * g.astype(F32)).astype(x.dtype)


def _heads(a, n, d):
    b, t, _ = a.shape
    return a.reshape(b, t, n, d).transpose(0, 2, 1, 3)


def _diff_heads(a):
    b, t, _ = a.shape
    return a.reshape(b, t, DA_HEADS, 2, DA_QK_DIM).transpose(0, 2, 3, 1, 4)


def _merge_heads(a):
    b, h, t, d = a.shape
    return a.transpose(0, 2, 1, 3).reshape(b, t, h * d)


def _split_in(p):
    out = {}
    o = 0
    for name, size in IN_SIZES:
        out[name] = p[..., o:o + size]
        o += size
    return out


def _flip(a):
    return jnp.flip(a, axis=2)


def rope_2d(x, row, col):
    half = x.shape[-1] // 2
    nf = half // 2
    inv = ROPE_THETA ** (-jnp.arange(nf, dtype=F32) / nf)

    def rot(xp, pos):
        ang = pos.astype(F32)[:, None] * inv[None, :]
        cos, sin = jnp.cos(ang), jnp.sin(ang)
        x1 = xp[..., :nf].astype(F32)
        x2 = xp[..., nf:].astype(F32)
        return jnp.concatenate([x1 * cos - x2 * sin, x2 * cos + x1 * sin], axis=-1)

    return jnp.concatenate([rot(x[..., :half], row), rot(x[..., half:], col)], axis=-1).astype(x.dtype)


def dense_attention(q, k, v):
    s = jnp.einsum('bhqd,bhkd->bhqk', q, k).astype(F32) * (q.shape[-1] ** -0.5)
    p = jax.nn.softmax(s, axis=-1).astype(v.dtype)
    return jnp.einsum('bhqk,bhkd->bhqd', p, v)


def neighborhood_attention(q, k, v, k_ctx, v_ctx, rpb):
    bn, h, s_len, dh = q.shape
    rows = s_len // GRID_W
    kr = min(NA_WIN_ROWS, rows)
    kc = NA_WIN_COLS
    scale = dh ** -0.5
    qg = q.reshape(bn, h, rows, GRID_W, dh)
    kg = k.reshape(bn, h, rows, GRID_W, dh)
    vg = v.reshape(bn, h, rows, GRID_W, dh)
    r = jnp.arange(rows)
    r0 = jnp.clip(r - kr // 2, 0, rows - kr)
    row_idx = r0[:, None] + jnp.arange(kr)[None, :]
    k_rows = jnp.take(kg, row_idx, axis=2)
    v_rows = jnp.take(vg, row_idx, axis=2)
    w = jnp.arange(GRID_W)
    c0 = jnp.clip(w - kc // 2, 0, GRID_W - kc)
    valid = (w[None, :] >= c0[:, None]) & (w[None, :] < c0[:, None] + kc)
    col_off = w[None, :] - w[:, None]
    ro = (row_idx - r[:, None]) + NA_WIN_ROWS - 1
    co = jnp.clip(col_off, -(NA_WIN_COLS - 1), NA_WIN_COLS - 1) + NA_WIN_COLS - 1
    bias = rpb[:, ro[:, None, :, None], co[None, :, None, :]].astype(F32)
    bias = jnp.where(valid[None, None, :, None, :], bias, NEG_INF)
    s_win = jnp.einsum('bhrwd,bhricd->bhrwic', qg, k_rows).astype(F32) * scale + bias[None]
    s_ctx = jnp.einsum('bhrwd,bhld->bhrwl', qg, k_ctx).astype(F32) * scale
    n_win = kr * GRID_W
    s_all = jnp.concatenate([s_win.reshape(bn, h, rows, GRID_W, n_win), s_ctx], axis=-1)
    p = jax.nn.softmax(s_all, axis=-1).astype(v.dtype)
    p_win = p[..., :n_win].reshape(bn, h, rows, GRID_W, kr, GRID_W)
    p_ctx = p[..., n_win:]
    o = (jnp.einsum('bhrwic,bhricd->bhrwd', p_win, v_rows)
         + jnp.einsum('bhrwl,bhld->bhrwd', p_ctx, v_ctx))
    return o.reshape(bn, h, s_len, dh)


def diff_weights(q, k, lam):
    s = jnp.einsum('bhcqd,bhckd->bhcqk', q, k).astype(F32) * (q.shape[-1] ** -0.5)
    p = jax.nn.softmax(s, axis=-1)
    return p[:, :, 0] - lam * p[:, :, 1]


def diff_attention_latent(q, k_all, v_all, lam):
    bn, h, _, s_len, d = q.shape
    nb = s_len // Q_BLOCK
    qb = q.reshape(bn, h, 2, nb, Q_BLOCK, d).transpose(3, 0, 1, 2, 4, 5)

    def block(qi):
        wts = diff_weights(qi, k_all, lam).astype(v_all.dtype)
        return jnp.einsum('bhqk,bhkv->bhqv', wts, v_all)

    o = lax.map(block, qb)
    return o.transpose(1, 2, 0, 3, 4).reshape(bn, h, s_len, v_all.shape[-1])


def gla_scan(q, k, v, g, s0):
    bn, h, t_len, dk = q.shape
    n = t_len // GLA_CHUNK
    tri = jnp.tril(jnp.ones((GLA_CHUNK, GLA_CHUNK), dtype=bool))

    def chunks(a):
        return a.reshape(bn, h, n, GLA_CHUNK, a.shape[-1]).transpose(2, 0, 1, 3, 4)

    def step(state, inp):
        qc, kc, vc, gc = inp
        b = jnp.cumsum(gc, axis=2)
        b_last = b[:, :, -1:, :]
        dec = jnp.where(tri[None, None, :, :, None], b[:, :, :, None, :] - b[:, :, None, :, :], -jnp.inf)
        attn = jnp.einsum('bhtd,bhsd,bhtsd->bhts', qc, kc, jnp.exp(dec))
        o = (jnp.einsum('bhts,bhsv->bhtv', attn, vc)
             + jnp.einsum('bhtd,bhdv->bhtv', qc * jnp.exp(b), state))
        state = (jnp.exp(b_last[:, :, 0, :])[..., None] * state
                 + jnp.einsum('bhsd,bhsv->bhdv', kc * jnp.exp(b_last - b), vc))
        return state, o

    s_fin, o = lax.scan(step, s0, (chunks(q), chunks(k), chunks(v), chunks(g)))
    return o.transpose(1, 2, 0, 3, 4).reshape(bn, h, t_len, v.shape[-1]), s_fin


def _gla_inputs(pp, lp):
    q = _heads(pp['qc'], GLA_HEADS, GLA_DK).astype(F32) * (GLA_DK ** -0.5)
    k = _heads(pp['kc'], GLA_HEADS, GLA_DK).astype(F32)
    v = _heads(pp['vc'], GLA_HEADS, GLA_DV).astype(F32)
    gf = jax.nn.log_sigmoid((pp['af'] @ lp['w_a2_f'] + lp['b_a_f']).astype(F32)) / GLA_GATE_NORM
    gb = jax.nn.log_sigmoid((pp['ab'] @ lp['w_a2_b'] + lp['b_a_b']).astype(F32)) / GLA_GATE_NORM
    return q, k, v, _heads(gf, GLA_HEADS, GLA_DK), _heads(gb, GLA_HEADS, GLA_DK)


def _gla_out(o, gate, g_norm):
    o = rms_norm(o.astype(gate.dtype), g_norm)
    return o * jax.nn.silu(_heads(gate, GLA_HEADS, GLA_DV))


def conv_ffn(xn, w_g, w_u, conv_w, conv_b, w_d):
    t_len = xn.shape[1]
    a = xn @ w_g
    ap = jnp.pad(a, ((0, 0), (CONV_W // 2, CONV_W // 2), (0, 0)))
    a = conv_b + sum(ap[:, j:j + t_len] * conv_w[j] for j in range(CONV_W))
    return (jax.nn.silu(a) * (xn @ w_u)) @ w_d


def _mixer(xn, xn_c, lp, lam_init, with_ctx_out):
    bn, s_len, _ = xn.shape
    t = jnp.arange(s_len)
    row, col = t // GRID_W, t % GRID_W
    pl = _split_in(xn @ lp['w_in'])
    pc = _split_in(xn_c @ lp['w_in'])

    qa = rms_norm(_heads(pl['qa'], NA_HEADS, HEAD_DIM), lp['qn_a'])
    ka = rms_norm(_heads(pl['ka'], NA_HEADS, HEAD_DIM), lp['kn_a'])
    va = _heads(pl['va'], NA_HEADS, HEAD_DIM)
    ka_c = rms_norm(_heads(pc['ka'], NA_HEADS, HEAD_DIM), lp['kn_a'])
    va_c = _heads(pc['va'], NA_HEADS, HEAD_DIM)
    o_a = neighborhood_attention(qa, ka, va, ka_c, va_c, lp['rpb_a'])

    lam = (jnp.exp(jnp.sum((lp['lam_q1'] * lp['lam_k1']).astype(F32)))
           - jnp.exp(jnp.sum((lp['lam_q2'] * lp['lam_k2']).astype(F32))) + lam_init)
    qb = rope_2d(rms_norm(_diff_heads(pl['qb']), lp['qn_b']), row, col)
    kb = rope_2d(rms_norm(_diff_heads(pl['kb']), lp['kn_b']), row, col)
    vb = _heads(pl['vb'], DA_HEADS, DA_V_DIM)
    kb_c = rms_norm(_diff_heads(pc['kb']), lp['kn_b'])
    vb_c = _heads(pc['vb'], DA_HEADS, DA_V_DIM)
    o_b = diff_attention_latent(qb, jnp.concatenate([kb, kb_c], axis=3),
                                jnp.concatenate([vb, vb_c], axis=2), lam)
    o_b = rms_norm(o_b, lp['subln_b']) * (1.0 - lam_init)

    qc, kc, vc, gf, gb = _gla_inputs(pl, lp)
    qc_c, kc_c, vc_c, gf_c, gb_c = _gla_inputs(pc, lp)
    s0 = jnp.zeros((bn, GLA_HEADS, GLA_DK, GLA_DV), F32)
    o_cf, s_f = gla_scan(qc_c, kc_c, vc_c, gf_c, s0)
    o_cb, s_b = gla_scan(_flip(qc_c), _flip(kc_c), _flip(vc_c), _flip(gb_c), s0)
    o_lf, _ = gla_scan(qc, kc, vc, gf, s_f)
    o_lb, _ = gla_scan(_flip(qc), _flip(kc), _flip(vc), _flip(gb), s_b)
    o_c = _gla_out(o_lf + _flip(o_lb), pl['gc'], lp['onorm_c'])

    y = jnp.concatenate([_merge_heads(o_a), _merge_heads(o_b), _merge_heads(o_c)], axis=-1) @ lp['w_out']
    if not with_ctx_out:
        return y, None

    qa_c = rms_norm(_heads(pc['qa'], NA_HEADS, HEAD_DIM), lp['qn_a'])
    o_a_c = dense_attention(qa_c, ka_c, va_c)
    qb_c = rms_norm(_diff_heads(pc['qb']), lp['qn_b'])
    wts_c = diff_weights(qb_c, kb_c, lam).astype(vb_c.dtype)
    o_b_c = rms_norm(jnp.einsum('bhqk,bhkv->bhqv', wts_c, vb_c), lp['subln_b']) * (1.0 - lam_init)
    o_c_c = _gla_out(o_cf + _flip(o_cb), pc['gc'], lp['onorm_c'])
    y_c = jnp.concatenate([_merge_heads(o_a_c), _merge_heads(o_b_c), _merge_heads(o_c_c)], axis=-1) @ lp['w_out']
    return y, y_c


def setup_inputs(seed: int = 0) -> dict:
    key = jax.random.key(seed)
    ks = iter(jax.random.split(key, 32))
    L, D = DEPTH, D_MODEL

    def nrm(shape, s):
        return jax.random.normal(next(ks), shape, F32) * s

    def gain(shape):
        return 1.0 + nrm(shape, 0.02)

    return {
        'x': nrm((BATCH, SEQ, D), 1.0),
        'c': nrm((BATCH, D), 1.0),
        'ctx': nrm((BATCH, CTX_LEN, D), 1.0),
        'c_ctx': nrm((D,), 1.0),
        'norm1': gain((L, D)),
        'norm2': gain((L, D)),
        'w_ada': nrm((L, D, 6 * D), 0.5 * D ** -0.5),
        'b_ada': nrm((L, 6 * D), 0.02),
        'w_in': nrm((L, D, IN_W), D ** -0.5),
        'qn_a': gain((L, HEAD_DIM)),
        'kn_a': gain((L, HEAD_DIM)),
        'rpb_a': nrm((L, NA_HEADS, 2 * NA_WIN_ROWS - 1, 2 * NA_WIN_COLS - 1), 0.1),
        'qn_b': gain((L, DA_QK_DIM)),
        'kn_b': gain((L, DA_QK_DIM)),
        'lam_q1': nrm((L, DA_QK_DIM), 0.1),
        'lam_k1': nrm((L, DA_QK_DIM), 0.1),
        'lam_q2': nrm((L, DA_QK_DIM), 0.1),
        'lam_k2': nrm((L, DA_QK_DIM), 0.1),
        'subln_b': gain((L, DA_V_DIM)),
        'w_a2_f': nrm((L, GLA_GATE_RANK, GLA_K_W), GLA_GATE_RANK ** -0.5),
        'b_a_f': nrm((L, GLA_K_W), 0.1),
        'w_a2_b': nrm((L, GLA_GATE_RANK, GLA_K_W), GLA_GATE_RANK ** -0.5),
        'b_a_b': nrm((L, GLA_K_W), 0.1),
        'onorm_c': gain((L, GLA_DV)),
        'w_out': nrm((L, MIX_W, D), MIX_W ** -0.5),
        'w_g': nrm((L, D, FFN_DIM), D ** -0.5),
        'w_u': nrm((L, D, FFN_DIM), D ** -0.5),
        'conv_w': nrm((L, CONV_W, FFN_DIM), CONV_W ** -0.5),
        'conv_b': nrm((L, FFN_DIM), 0.02),
        'w_d': nrm((L, FFN_DIM, D), FFN_DIM ** -0.5),
    }


def reference(x, c, ctx, c_ctx, norm1, norm2, w_ada, b_ada, w_in, qn_a, kn_a, rpb_a,
              qn_b, kn_b, lam_q1, lam_k1, lam_q2, lam_k2, subln_b, w_a2_f, b_a_f,
              w_a2_b, b_a_b, onorm_c, w_out, w_g, w_u, conv_w, conv_b, w_d):
    h, hc = x, ctx
    for l in range(DEPTH):
        with_ctx_out = l < DEPTH - 1
        lam_init = 0.8 - 0.6 * math.exp(-0.3 * l)
        lp = {'w_in': w_in[l], 'qn_a': qn_a[l], 'kn_a': kn_a[l], 'rpb_a': rpb_a[l],
              'qn_b': qn_b[l], 'kn_b': kn_b[l], 'lam_q1': lam_q1[l], 'lam_k1': lam_k1[l],
              'lam_q2': lam_q2[l], 'lam_k2': lam_k2[l], 'subln_b': subln_b[l],
              'w_a2_f': w_a2_f[l], 'b_a_f': b_a_f[l], 'w_a2_b': w_a2_b[l], 'b_a_b': b_a_b[l],
              'onorm_c': onorm_c[l], 'w_out': w_out[l]}
        sh1, sc1, g1, sh2, sc2, g2 = jnp.split((jax.nn.silu(c) @ w_ada[l] + b_ada[l])[:, None, :], 6, axis=-1)
        csh1, csc1, cg1, csh2, csc2, cg2 = jnp.split(jax.nn.silu(c_ctx) @ w_ada[l] + b_ada[l], 6, axis=-1)
        xn = rms_norm(h, norm1[l]) * (1.0 + sc1) + sh1
        xn_c = rms_norm(hc, norm1[l]) * (1.0 + csc1) + csh1
        y, y_c = _mixer(xn, xn_c, lp, lam_init, with_ctx_out)
        h = h + g1 * y
        h = h + g2 * conv_ffn(rms_norm(h, norm2[l]) * (1.0 + sc2) + sh2,
                              w_g[l], w_u[l], conv_w[l], conv_b[l], w_d[l])
        if with_ctx_out:
            hc = hc + cg1 * y_c
            hc = hc + cg2 * conv_ffn(rms_norm(hc, norm2[l]) * (1.0 + csc2) + csh2,
                                     w_g[l], w_u[l], conv_w[l], conv_b[l], w_d[l])
    return h
```

```python
import functools
import math

import jax
import jax.numpy as jnp
import numpy as np
from jax import lax
from jax.experimental import pallas as pl
from jax.experimental.pallas import tpu as pltpu

F32 = jnp.float32
BF16 = jnp.bfloat16

D_MODEL = 1024
GRID_W = 64
HEAD_W = 64
N_HEADS = 4
NA_WIN_ROWS = 8
NA_WIN_COLS = 16
NA_Q_ROWS = 4
NA_K_ROWS = 12
DA_QK_DIM = 32
GLA_DK = 64
GLA_DV = 128
GLA_CHUNK = 64
GLA_GATE_RANK = 16
GLA_GATE_NORM = 16.0
GATE_PAD = 128
FFN_DIM = 2816
ROPE_THETA = 10000.0
EPS = 1e-6
NEG_INF = -1e30
QK_W = N_HEADS * HEAD_W
GLA_V_W = N_HEADS * GLA_DV
VMEM_LIMIT = 56 * 1024 * 1024

_NT = (((1,), (1,)), ((), ()))


def _cparams(n_axes):
    return pltpu.CompilerParams(dimension_semantics=("arbitrary",) * n_axes,
                                vmem_limit_bytes=VMEM_LIMIT)


def _split_bf16(x, parts):
    out = []
    r = x
    for _ in range(parts):
        p = r.astype(BF16)
        out.append(p)
        r = r - p.astype(F32)
    return out


def _dot_f32_lhs(x, m_bf16, parts):
    acc = None
    for p in _split_bf16(x, parts):
        t = jnp.dot(p, m_bf16, preferred_element_type=F32)
        acc = t if acc is None else acc + t
    return acc


def _dot_f32_rhs(m_bf16, x, parts):
    acc = None
    for p in _split_bf16(x, parts):
        t = jnp.dot(m_bf16, p, preferred_element_type=F32)
        acc = t if acc is None else acc + t
    return acc


def _silu(x):
    return x * (1.0 / (1.0 + jnp.exp(-x)))


def _log_sigmoid(x):
    return jnp.minimum(x, 0.0) - jnp.log(1.0 + jnp.exp(-jnp.abs(x)))


def _ada_kernel(c_ref, w_ref, b_ref, o_ref):
    s = _silu(c_ref[...])
    w = w_ref[0]
    acc = None
    for sp in _split_bf16(s, 3):
        for wp in _split_bf16(w, 2):
            t = jnp.dot(sp, wp, preferred_element_type=F32)
            acc = t if acc is None else acc + t
    o_ref[0] = acc + b_ref[0]


def _ada(cvec, w_ada, b_ada):
    n_l, d, n6 = w_ada.shape
    tn = 1536
    return pl.pallas_call(
        _ada_kernel,
        grid=(n_l, n6 // tn),
        in_specs=[pl.BlockSpec((16, d), lambda l, j: (0, 0)),
                  pl.BlockSpec((1, d, tn), lambda l, j: (l, 0, j)),
                  pl.BlockSpec((1, 1, tn), lambda l, j: (l, 0, j))],
        out_specs=pl.BlockSpec((1, 16, tn), lambda l, j: (l, 0, j)),
        out_shape=jax.ShapeDtypeStruct((n_l, 16, n6), F32),
        compiler_params=_cparams(2),
        name="ada_proj",
    )(cvec, w_ada, b_ada.reshape(n_l, 1, n6))


def _group_rms(x, gmat, gain, gsz):
    ssq = _dot_f32_lhs(x * x, gmat, 2)
    return x * lax.rsqrt(ssq * (1.0 / gsz) + EPS) * gain


def _inproj_kernel(h_ref, sc_ref, sh_ref, n1_ref, wn_ref, wt_ref, g64_ref, g32_ref, gains_ref,
                   w2_ref, b2_ref, *rest, rope):
    if rope:
        cos_ref, s1_ref, s2_ref = rest[:3]
        rest = rest[3:]
    (qa_ref, ka_ref, qb_ref, kb_ref, qc_ref, kc_ref, vc_ref, gc_ref, gf_ref, gb_ref,
     vat_ref, vbt_ref) = rest

    x = h_ref[0]
    y = x * lax.rsqrt(jnp.mean(x * x, axis=-1, keepdims=True) + EPS) * n1_ref[...]
    xn = (y * (1.0 + sc_ref[0]) + sh_ref[0]).astype(BF16)

    def proj(lo, hi):
        return jnp.dot(xn, wn_ref[:, lo:hi], preferred_element_type=F32)

    g64 = g64_ref[...]
    g32 = g32_ref[...]
    gains = gains_ref[...]

    def rot(v):
        if not rope:
            return v
        return (v * cos_ref[...] + pltpu.roll(v, QK_W - 8, 1) * s1_ref[...]
                + pltpu.roll(v, 8, 1) * s2_ref[...])

    qa_ref[0] = (_group_rms(proj(0, 256), g64, gains[0:1], HEAD_W) * (HEAD_W ** -0.5)).astype(BF16)
    ka_ref[0] = _group_rms(proj(256, 512), g64, gains[1:2], HEAD_W).astype(BF16)
    qb_ref[0] = (rot(_group_rms(proj(512, 768), g32, gains[2:3], DA_QK_DIM))
                 * (DA_QK_DIM ** -0.5)).astype(BF16)
    kb_ref[0] = rot(_group_rms(proj(768, 1024), g32, gains[3:4], DA_QK_DIM)).astype(BF16)
    qc_ref[0] = proj(1024, 1280) * (GLA_DK ** -0.5)
    kc_ref[0] = proj(1280, 1536)
    vc_ref[0] = proj(1536, 2048).astype(BF16)
    gc_ref[0] = proj(2048, 2560)
    a_lr = proj(2560, 2560 + GATE_PAD)
    pre = None
    for ap in _split_bf16(a_lr, 2):
        for wp in (w2_ref[0], w2_ref[1]):
            t = jnp.dot(ap, wp, preferred_element_type=F32)
            pre = t if pre is None else pre + t
    gate = _log_sigmoid(pre + b2_ref[...]) * (1.0 / GLA_GATE_NORM)
    gf_ref[0] = gate[:, :QK_W]
    gb_ref[0] = gate[:, QK_W:]
    vt = lax.dot_general(wt_ref[...], xn, _NT, preferred_element_type=F32)
    vat_ref[0] = vt[:QK_W].astype(BF16)
    vbt_ref[0] = vt[QK_W:].astype(BF16)


def _inproj(h, sc, sh, n1, wts, rope_tabs, tm):
    b, t, d = h.shape
    nt = t // tm
    rope = rope_tabs is not None
    const = lambda shape: pl.BlockSpec(shape, lambda i, bb: (0,) * len(shape))
    tok = lambda w: pl.BlockSpec((1, tm, w), lambda i, bb: (bb, i, 0))
    mod = pl.BlockSpec((1, 1, d), lambda i, bb: (bb, 0, 0))
    in_specs = [tok(d), mod, mod, const((1, d)), const(wts["wn"].shape), const(wts["wt"].shape),
                const((QK_W, QK_W)), const((QK_W, QK_W)), const((4, QK_W)),
                const((2, GATE_PAD, 2 * QK_W)), const((1, 2 * QK_W))]
    args = [h, sc, sh, n1, wts["wn"], wts["wt"], wts["g64"], wts["g32"], wts["gains"],
            wts["w2"], wts["b2"]]
    if rope:
        in_specs += [pl.BlockSpec((tm, QK_W), lambda i, bb: (i, 0))] * 3
        args += list(rope_tabs)
    tspec = pl.BlockSpec((1, QK_W, tm), lambda i, bb: (bb, 0, i))
    out_specs = [tok(QK_W)] * 6 + [tok(GLA_V_W), tok(GLA_V_W), tok(QK_W), tok(QK_W), tspec, tspec]
    sd = jax.ShapeDtypeStruct
    out_shape = [sd((b, t, QK_W), BF16)] * 4 + [sd((b, t, QK_W), F32)] * 2 + [
        sd((b, t, GLA_V_W), BF16), sd((b, t, GLA_V_W), F32), sd((b, t, QK_W), F32),
        sd((b, t, QK_W), F32), sd((b, QK_W, t), BF16), sd((b, QK_W, t), BF16)]
    outs = pl.pallas_call(
        functools.partial(_inproj_kernel, rope=rope),
        grid=(nt, b), in_specs=in_specs, out_specs=out_specs, out_shape=out_shape,
        compiler_params=_cparams(2), name="in_proj_rope" if rope else "in_proj",
    )(*args)
    names = ("qa", "ka", "qb", "kb", "qc", "kc", "vc", "gc", "gf", "gb", "vat", "vbt")
    return dict(zip(names, outs))


def _head_mask(shape, lo, hi):
    lane = lax.broadcasted_iota(jnp.int32, shape, 1)
    return (lane >= lo) & (lane < hi)


def _na_kernel(q_ref, k_ref, vt_ref, kc_ref, vct_ref, bias_ref, o_ref, *, n_blocks, max_base):
    i = pl.program_id(1)
    nq = NA_Q_ROWS * GRID_W
    nk = NA_K_ROWS * GRID_W
    base = jnp.clip(NA_Q_ROWS * i - NA_WIN_ROWS // 2, 0, max_base) * GRID_W
    base = pl.multiple_of(base, 256)
    kw = k_ref[0, pl.ds(base, nk), :]
    vw = vt_ref[0, :, pl.ds(base, nk)]
    kc = kc_ref[0]
    vct = vct_ref[0]
    q = q_ref[0]
    zero = jnp.zeros_like(q)
    for h in range(N_HEADS):
        qm = jnp.where(_head_mask(q.shape, h * HEAD_W, (h + 1) * HEAD_W), q, zero)
        s = lax.dot_general(kw, qm, _NT, preferred_element_type=F32) + bias_ref[h, 0]
        sc = lax.dot_general(kc, qm, _NT, preferred_element_type=F32)
        m = jnp.maximum(jnp.max(s, axis=0, keepdims=True), jnp.max(sc, axis=0, keepdims=True))
        p = jnp.exp(s - m)
        pc = jnp.exp(sc - m)
        l = jnp.sum(p, axis=0, keepdims=True) + jnp.sum(pc, axis=0, keepdims=True)
        rows = slice(h * HEAD_W, (h + 1) * HEAD_W)
        ot = (jnp.dot(vw[rows], p.astype(BF16), preferred_element_type=F32)
              + jnp.dot(vct[rows], pc.astype(BF16), preferred_element_type=F32))
        o_ref[0, :, rows] = (ot / l).T.astype(BF16)


def _na_bias(rpb, rows):
    nb = rows // NA_Q_ROWS
    max_base = rows - NA_K_ROWS
    ro_l, co_l, ok_l = [], [], []
    w = np.arange(GRID_W)
    c0 = np.clip(w - NA_WIN_COLS // 2, 0, GRID_W - NA_WIN_COLS)
    for i in (0, 1, nb - 1):
        base = int(np.clip(NA_Q_ROWS * i - NA_WIN_ROWS // 2, 0, max_base))
        key_row = base + np.arange(NA_K_ROWS)
        r = NA_Q_ROWS * i + np.arange(NA_Q_ROWS)
        r0 = np.clip(r - NA_WIN_ROWS // 2, 0, rows - NA_WIN_ROWS)
        ok_row = (key_row[:, None] >= r0[None, :]) & (key_row[:, None] < r0[None, :] + NA_WIN_ROWS)
        ro = key_row[:, None] - r[None, :] + NA_WIN_ROWS - 1
        ok_col = (w[:, None] >= c0[None, :]) & (w[:, None] < c0[None, :] + NA_WIN_COLS)
        co = np.clip(w[:, None] - w[None, :], -(NA_WIN_COLS - 1), NA_WIN_COLS - 1) + NA_WIN_COLS - 1
        shape = (NA_K_ROWS, GRID_W, NA_Q_ROWS, GRID_W)
        ro_l.append(np.broadcast_to(np.clip(ro, 0, 2 * NA_WIN_ROWS - 2)[:, None, :, None], shape))
        co_l.append(np.broadcast_to(co[None, :, None, :], shape))
        ok_l.append(np.broadcast_to(ok_row[:, None, :, None] & ok_col[None, :, None, :], shape))
    nk, nq = NA_K_ROWS * GRID_W, NA_Q_ROWS * GRID_W
    ro = np.stack(ro_l).reshape(3, nk, nq)
    co = np.stack(co_l).reshape(3, nk, nq)
    ok = np.stack(ok_l).reshape(3, nk, nq)
    return jnp.where(ok[None], rpb[:, ro, co].astype(F32), NEG_INF)


def _na_attention(q, k, vt, kc, vct, bias):
    b, s, _ = q.shape
    rows = s // GRID_W
    nb = rows // NA_Q_ROWS
    nq, nk = NA_Q_ROWS * GRID_W, NA_K_ROWS * GRID_W
    lc = kc.shape[1]
    btype = lambda bb, i: (0, jnp.where(i == 0, 0, jnp.where(i == nb - 1, 2, 1)), 0, 0)
    return pl.pallas_call(
        functools.partial(_na_kernel, n_blocks=nb, max_base=rows - NA_K_ROWS),
        grid=(b, nb),
        in_specs=[pl.BlockSpec((1, nq, QK_W), lambda bb, i: (bb, i, 0)),
                  pl.BlockSpec((1, s, QK_W), lambda bb, i: (bb, 0, 0)),
                  pl.BlockSpec((1, QK_W, s), lambda bb, i: (bb, 0, 0)),
                  pl.BlockSpec((1, lc, QK_W), lambda bb, i: (bb, 0, 0)),
                  pl.BlockSpec((1, QK_W, lc), lambda bb, i: (bb, 0, 0)),
                  pl.BlockSpec((N_HEADS, 1, nk, nq), btype)],
        out_specs=pl.BlockSpec((1, nq, QK_W), lambda bb, i: (bb, i, 0)),
        out_shape=jax.ShapeDtypeStruct((b, s, QK_W), BF16),
        compiler_params=_cparams(2), name="na_attention",
    )(q, k, vt, kc, vct, bias)


def _flash_kernel(*refs, n_comp, tk, n_chunks, has_extra, lam_init):
    q_ref, k_ref, vt_ref = refs[:3]
    refs = refs[3:]
    if has_extra:
        kx_ref, vxt_ref = refs[:2]
        refs = refs[2:]
    if n_comp == 2:
        lam_ref, gain_ref = refs[:2]
        refs = refs[2:]
    (o_ref,) = refs

    q = q_ref[0]
    tq = q.shape[0]
    dsub = HEAD_W // n_comp
    zero = jnp.zeros_like(q)
    if n_comp == 2:
        lv = lam_ref[...]
        lam = (jnp.exp(jnp.sum(lv[0:1] * lv[1:2], axis=1, keepdims=True))
               - jnp.exp(jnp.sum(lv[2:3] * lv[3:4], axis=1, keepdims=True)) + lam_init)

    for h in range(N_HEADS):
        rows = slice(h * HEAD_W, (h + 1) * HEAD_W)
        qm = jnp.concatenate(
            [jnp.where(_head_mask(q.shape, h * HEAD_W + c * dsub, h * HEAD_W + (c + 1) * dsub), q, zero)
             for c in range(n_comp)], axis=0)

        def chunk(kc, vtc, carry):
            s = lax.dot_general(kc, qm, _NT, preferred_element_type=F32)
            new = []
            for c in range(n_comp):
                m, l, acc = carry[c]
                sc = s[:, c * tq:(c + 1) * tq]
                m_new = jnp.maximum(m, jnp.max(sc, axis=0, keepdims=True))
                alpha = jnp.exp(m - m_new)
                p = jnp.exp(sc - m_new)
                l = alpha * l + jnp.sum(p, axis=0, keepdims=True)
                acc = alpha * acc + jnp.dot(vtc, p.astype(BF16), preferred_element_type=F32)
                new.append((m_new, l, acc))
            return tuple(new)

        init = tuple((jnp.full((1, tq), NEG_INF, F32), jnp.zeros((1, tq), F32),
                      jnp.zeros((HEAD_W, tq), F32)) for _ in range(n_comp))

        def body(j, carry):
            off = pl.multiple_of(j * tk, tk)
            return chunk(k_ref[0, pl.ds(off, tk), :], vt_ref[0, rows, pl.ds(off, tk)], carry)

        carry = lax.fori_loop(0, n_chunks, body, init)
        if has_extra:
            carry = chunk(kx_ref[0], vxt_ref[0, rows, :], carry)
        if n_comp == 2:
            (_, l0, a0), (_, l1, a1) = carry
            o = a0 / l0 - lam * (a1 / l1)
            o = (o * lax.rsqrt(jnp.mean(o * o, axis=0, keepdims=True) + EPS)
                 * gain_ref[...] * (1.0 - lam_init))
        else:
            ((_, l0, a0),) = carry
            o = a0 / l0
        o_ref[0, :, rows] = o.T.astype(BF16)


def _flash(q, k, vt, extra, diff, tq, tk, lam_init=0.0):
    b, t, _ = q.shape
    tkeys = k.shape[1]
    in_specs = [pl.BlockSpec((1, tq, QK_W), lambda bb, i: (bb, i, 0)),
                pl.BlockSpec((1, tkeys, QK_W), lambda bb, i: (bb, 0, 0)),
                pl.BlockSpec((1, QK_W, tkeys), lambda bb, i: (bb, 0, 0))]
    args = [q, k, vt]
    if extra is not None:
        lx = extra[0].shape[1]
        in_specs += [pl.BlockSpec((1, lx, QK_W), lambda bb, i: (bb, 0, 0)),
                     pl.BlockSpec((1, QK_W, lx), lambda bb, i: (bb, 0, 0))]
        args += list(extra)
    if diff is not None:
        in_specs += [pl.BlockSpec((4, DA_QK_DIM), lambda bb, i: (0, 0)),
                     pl.BlockSpec((HEAD_W, 1), lambda bb, i: (0, 0))]
        args += list(diff)
    return pl.pallas_call(
        functools.partial(_flash_kernel, n_comp=2 if diff is not None else 1, tk=tk,
                          n_chunks=tkeys // tk, has_extra=extra is not None, lam_init=lam_init),
        grid=(b, t // tq), in_specs=in_specs,
        out_specs=pl.BlockSpec((1, tq, QK_W), lambda bb, i: (bb, i, 0)),
        out_shape=jax.ShapeDtypeStruct((b, t, QK_W), BF16),
        compiler_params=_cparams(2),
        name="diff_attention" if diff is not None else "dense_attention",
    )(*args)


def _gla_dir(q_ref, k_ref, v_ref, g_ref, o_ref, state, cum_ref, order, reverse, mask64):
    c = GLA_CHUNK
    g_all = g_ref[0]
    b_all = _dot_f32_rhs(cum_ref[...], g_all, 3)
    for ci in order:
        r0 = ci * c
        bch = b_all[r0:r0 + c]
        qch = q_ref[0, r0:r0 + c, :]
        kch = k_ref[0, r0:r0 + c, :]
        vch = v_ref[0, r0:r0 + c, :]
        btot = bch[0:1] if reverse else bch[c - 1:c]
        bmid = bch[c // 2:c // 2 + 1]
        q_in = (qch * jnp.exp(bch - bmid)).astype(BF16)
        k_in = (kch * jnp.exp(bmid - bch)).astype(BF16)
        q_st = (qch * jnp.exp(bch)).astype(BF16)
        k_st = kch * jnp.exp(btot - bch)
        k_st_t = k_st.T.astype(BF16)
        dec = jnp.broadcast_to(jnp.exp(btot), (GLA_DV, QK_W)).T
        s_old = state[...]
        s_bf = s_old.astype(BF16)
        upd = []
        for h in range(N_HEADS):
            kr = slice(h * GLA_DK, (h + 1) * GLA_DK)
            vr = slice(h * GLA_DV, (h + 1) * GLA_DV)
            attn = lax.dot_general(q_in[:, kr], k_in[:, kr], _NT, preferred_element_type=F32)
            attn = jnp.where(mask64, attn, 0.0).astype(BF16)
            o = (jnp.dot(attn, vch[:, vr], preferred_element_type=F32)
                 + jnp.dot(q_st[:, kr], s_bf[kr], preferred_element_type=F32))
            o_ref[0, r0:r0 + c, vr] = o
            upd.append(jnp.dot(k_st_t[kr], vch[:, vr], preferred_element_type=F32))
        state[...] = dec * s_old + jnp.concatenate(upd, axis=0)


def _gla_kernel(qf_ref, kf_ref, vf_ref, gf_ref, qb_ref, kb_ref, vb_ref, gb_ref, s0_ref,
                cumf_ref, cumb_ref, of_ref, ob_ref, sfin_ref, sf, sb, *, n_chunks):
    i = pl.program_id(1)

    @pl.when(i == 0)
    def _():
        sf[...] = s0_ref[0, 0]
        sb[...] = s0_ref[0, 1]

    t = lax.broadcasted_iota(jnp.int32, (GLA_CHUNK, GLA_CHUNK), 0)
    s = lax.broadcasted_iota(jnp.int32, (GLA_CHUNK, GLA_CHUNK), 1)
    _gla_dir(qf_ref, kf_ref, vf_ref, gf_ref, of_ref, sf, cumf_ref, range(n_chunks), False, s <= t)
    _gla_dir(qb_ref, kb_ref, vb_ref, gb_ref, ob_ref, sb, cumb_ref, range(n_chunks - 1, -1, -1), True, s >= t)

    @pl.when(i == pl.num_programs(1) - 1)
    def _():
        sfin_ref[0, 0] = sf[...]
        sfin_ref[0, 1] = sb[...]


def _gla(p, s0, tb):
    b, t, _ = p["qc"].shape
    nb = t // tb
    n_chunks = tb // GLA_CHUNK
    idx = np.arange(tb)
    same = (idx[:, None] // GLA_CHUNK) == (idx[None, :] // GLA_CHUNK)
    cumf = jnp.asarray(same & (idx[None, :] <= idx[:, None]), BF16)
    cumb = jnp.asarray(same & (idx[None, :] >= idx[:, None]), BF16)
    fwd = lambda w: pl.BlockSpec((1, tb, w), lambda bb, i: (bb, i, 0))
    bwd = lambda w: pl.BlockSpec((1, tb, w), lambda bb, i: (bb, nb - 1 - i, 0))
    st = pl.BlockSpec((1, 2, QK_W, GLA_DV), lambda bb, i: (bb, 0, 0, 0))
    cm = pl.BlockSpec((tb, tb), lambda bb, i: (0, 0))
    sd = jax.ShapeDtypeStruct
    return pl.pallas_call(
        functools.partial(_gla_kernel, n_chunks=n_chunks),
        grid=(b, nb),
        in_specs=[fwd(QK_W), fwd(QK_W), fwd(GLA_V_W), fwd(QK_W),
                  bwd(QK_W), bwd(QK_W), bwd(GLA_V_W), bwd(QK_W), st, cm, cm],
        out_specs=[fwd(GLA_V_W), bwd(GLA_V_W), st],
        out_shape=[sd((b, t, GLA_V_W), F32), sd((b, t, GLA_V_W), F32), sd((b, 2, QK_W, GLA_DV), F32)],
        scratch_shapes=[pltpu.VMEM((QK_W, GLA_DV), F32), pltpu.VMEM((QK_W, GLA_DV), F32)],
        compiler_params=_cparams(2), name="gla_scan",
    )(p["qc"], p["kc"], p["vc"], p["gf"], p["qc"], p["kc"], p["vc"], p["gb"], s0, cumf, cumb)


def _outproj_kernel(h_ref, oa_ref, ob_ref, ocf_ref, ocb_ref, gc_ref, on_ref, w_ref, g1_ref, o_ref):
    oc = ocf_ref[0] + ocb_ref[0]
    gate = gc_ref[0]
    parts = []
    for h in range(N_HEADS):
        vr = slice(h * GLA_DV, (h + 1) * GLA_DV)
        x = oc[:, vr]
        x = x * lax.rsqrt(jnp.mean(x * x, axis=-1, keepdims=True) + EPS) * on_ref[...]
        parts.append((x * _silu(gate[:, vr])).astype(BF16))
    y = (jnp.dot(oa_ref[0], w_ref[0:QK_W], preferred_element_type=F32)
         + jnp.dot(ob_ref[0], w_ref[QK_W:2 * QK_W], preferred_element_type=F32)
         + jnp.dot(jnp.concatenate(parts, axis=1), w_ref[2 * QK_W:], preferred_element_type=F32))
    o_ref[0] = h_ref[0] + g1_ref[0] * y


def _outproj(h, oa, ob, ocf, ocb, gc, onorm, w_out, g1, tm):
    b, t, d = h.shape
    tok = lambda w: pl.BlockSpec((1, tm, w), lambda bb, i: (bb, i, 0))
    return pl.pallas_call(
        _outproj_kernel, grid=(b, t // tm),
        in_specs=[tok(d), tok(QK_W), tok(QK_W), tok(GLA_V_W), tok(GLA_V_W), tok(GLA_V_W),
                  pl.BlockSpec((1, GLA_DV), lambda bb, i: (0, 0)),
                  pl.BlockSpec(w_out.shape, lambda bb, i: (0, 0)),
                  pl.BlockSpec((1, 1, d), lambda bb, i: (bb, 0, 0))],
        out_specs=tok(d), out_shape=jax.ShapeDtypeStruct((b, t, d), F32),
        compiler_params=_cparams(2), name="out_proj",
    )(h, oa, ob, ocf, ocb, gc, onorm, w_out, g1)


FFN_HALO = 16


def _ffn_kernel(h_ref, hp_ref, hn_ref, sc_ref, sh_ref, n2_ref, wg_ref, wu_ref, wd_ref, cw_ref, cb_ref,
                g2_ref, o_ref, xn_scr, a_scr, *, fc):
    i = pl.program_id(1)
    nt = pl.num_programs(1)
    tm = h_ref.shape[1]
    n2 = n2_ref[...]
    sc = sc_ref[0]
    sh = sh_ref[0]

    def normed(x):
        y = x * lax.rsqrt(jnp.mean(x * x, axis=-1, keepdims=True) + EPS) * n2
        return y * (1.0 + sc) + sh

    x = h_ref[0]
    prev_ok = (i > 0).astype(F32)
    next_ok = (i < nt - 1).astype(F32)
    xn_scr[0:FFN_HALO] = (normed(hp_ref[0]) * prev_ok).astype(BF16)
    xn_scr[FFN_HALO:FFN_HALO + tm] = normed(x).astype(BF16)
    xn_scr[FFN_HALO + tm:] = (normed(hn_ref[0]) * next_ok).astype(BF16)

    acc = None
    for f0 in range(0, FFN_DIM, fc):
        a_scr[...] = jnp.dot(xn_scr[...], wg_ref[:, f0:f0 + fc], preferred_element_type=F32)
        u = jnp.dot(xn_scr[FFN_HALO:FFN_HALO + tm], wu_ref[:, f0:f0 + fc], preferred_element_type=F32)
        cw = cw_ref[:, f0:f0 + fc]
        a = (cb_ref[:, f0:f0 + fc]
             + a_scr[FFN_HALO - 1:FFN_HALO - 1 + tm] * cw[0:1]
             + a_scr[FFN_HALO:FFN_HALO + tm] * cw[1:2]
             + a_scr[FFN_HALO + 1:FFN_HALO + 1 + tm] * cw[2:3])
        g = (_silu(a) * u).astype(BF16)
        t = jnp.dot(g, wd_ref[f0:f0 + fc], preferred_element_type=F32)
        acc = t if acc is None else acc + t
    o_ref[0] = x + g2_ref[0] * acc


def _ffn(h, sc, sh, n2, wg, wu, wd, cw, cb, g2, tm, fc):
    b, t, d = h.shape
    nt = t // tm
    hb = tm // FFN_HALO
    tok = pl.BlockSpec((1, tm, d), lambda bb, i: (bb, i, 0))
    prev = pl.BlockSpec((1, FFN_HALO, d), lambda bb, i: (bb, jnp.maximum(i * hb - 1, 0), 0))
    nxt = pl.BlockSpec((1, FFN_HALO, d), lambda bb, i: (bb, jnp.minimum((i + 1) * hb, nt * hb - 1), 0))
    mod = pl.BlockSpec((1, 1, d), lambda bb, i: (bb, 0, 0))
    const = lambda shape: pl.BlockSpec(shape, lambda bb, i: (0,) * len(shape),
                                       pipeline_mode=pl.Buffered(1))
    return pl.pallas_call(
        functools.partial(_ffn_kernel, fc=fc), grid=(b, nt),
        in_specs=[tok, prev, nxt, mod, mod, const((1, d)), const(wg.shape), const(wu.shape),
                  const(wd.shape), const(cw.shape), const(cb.shape), mod],
        out_specs=tok, out_shape=jax.ShapeDtypeStruct((b, t, d), F32),
        scratch_shapes=[pltpu.VMEM((tm + 2 * FFN_HALO, d), BF16),
                        pltpu.VMEM((tm + 2 * FFN_HALO, fc), F32)],
        compiler_params=_cparams(2), name="conv_ffn",
    )(h, h, h, sc, sh, n2, wg, wu, wd, cw, cb, g2)


def _rope_tables(s):
    t = np.arange(s)
    row, col = t // GRID_W, t % GRID_W
    lane = np.arange(QK_W)
    jj = lane % DA_QK_DIM
    nf = DA_QK_DIM // 4
    inv = jnp.asarray(ROPE_THETA, F32) ** (-jnp.arange(nf, dtype=F32) / nf)
    pos = jnp.where((jj // (2 * nf) == 0)[None, :], row[:, None], col[:, None]).astype(F32)
    ang = pos * inv[jj % nf][None, :]
    first = ((jj % (2 * nf)) < nf)[None, :]
    cos, sin = jnp.cos(ang), jnp.sin(ang)
    return cos, jnp.where(first, -sin, 0.0), jnp.where(first, 0.0, sin)


def _block_ones(gsz):
    g = np.arange(QK_W) // gsz
    return jnp.asarray(g[:, None] == g[None, :], BF16)


def _layer_weights(l, w_in, qn_a, kn_a, qn_b, kn_b, w_a2_f, b_a_f, w_a2_b, b_a_b):
    w = w_in[l]
    d = w.shape[0]
    wn = jnp.concatenate([w[:, 0:512], w[:, 768:1280], w[:, 1536:3104],
                          jnp.zeros((d, GATE_PAD - 2 * GLA_GATE_RANK), F32)], axis=1).astype(BF16)
    wt = jnp.concatenate([w[:, 512:768], w[:, 1280:1536]], axis=1).T.astype(BF16)
    gains = jnp.stack([jnp.tile(qn_a[l], N_HEADS), jnp.tile(kn_a[l], N_HEADS),
                       jnp.tile(qn_b[l], 2 * N_HEADS), jnp.tile(kn_b[l], 2 * N_HEADS)])
    w2 = jnp.zeros((GATE_PAD, 2 * QK_W), F32)
    w2 = w2.at[0:GLA_GATE_RANK, 0:QK_W].set(w_a2_f[l])
    w2 = w2.at[GLA_GATE_RANK:2 * GLA_GATE_RANK, QK_W:].set(w_a2_b[l])
    w2_hi = w2.astype(BF16)
    w2_lo = (w2 - w2_hi.astype(F32)).astype(BF16)
    b2 = jnp.concatenate([b_a_f[l], b_a_b[l]])[None, :]
    return {"wn": wn, "wt": wt, "g64": _block_ones(HEAD_W), "g32": _block_ones(DA_QK_DIM),
            "gains": gains, "w2": jnp.stack([w2_hi, w2_lo]), "b2": b2}


def kernel(x, c, ctx, c_ctx, norm1, norm2, w_ada, b_ada, w_in, qn_a, kn_a, rpb_a, qn_b, kn_b,
           lam_q1, lam_k1, lam_q2, lam_k2, subln_b, w_a2_f, b_a_f, w_a2_b, b_a_b, onorm_c, w_out,
           w_g, w_u, conv_w, conv_b, w_d):
    bsz, s, d = x.shape
    lc = ctx.shape[1]
    depth = w_in.shape[0]
    rows = s // GRID_W
    tm = min(512, s)
    tb = min(512, s)

    cvec = jnp.zeros((16, d), F32).at[:bsz].set(c).at[bsz].set(c_ctx)
    ada = _ada(cvec, w_ada, b_ada)
    rope = _rope_tables(s)

    h, hc = x, ctx
    for l in range(depth):
        with_ctx_out = l < depth - 1
        lam_init = 0.8 - 0.6 * math.exp(-0.3 * l)
        m = ada[l, :bsz].reshape(bsz, 1, 6, d)
        mc = jnp.broadcast_to(ada[l, bsz].reshape(1, 1, 6, d), (bsz, 1, 6, d))
        sh1, sc1, g1, sh2, sc2, g2 = (m[:, :, j] for j in range(6))
        csh1, csc1, cg1, csh2, csc2, cg2 = (mc[:, :, j] for j in range(6))
        wts = _layer_weights(l, w_in, qn_a, kn_a, qn_b, kn_b, w_a2_f, b_a_f, w_a2_b, b_a_b)
        n1 = norm1[l][None, :]
        n2 = norm2[l][None, :]

        pl_ = _inproj(h, sc1, sh1, n1, wts, rope, tm)
        pc_ = _inproj(hc, csc1, csh1, n1, wts, None, lc)

        bias = _na_bias(rpb_a[l], rows)
        o_a = _na_attention(pl_["qa"], pl_["ka"], pl_["vat"], pc_["ka"], pc_["vat"], bias)
        lamv = jnp.stack([lam_q1[l], lam_k1[l], lam_q2[l], lam_k2[l]])
        diff = (lamv, subln_b[l][:, None])
        o_b = _flash(pl_["qb"], pl_["kb"], pl_["vbt"], (pc_["kb"], pc_["vbt"]), diff, 256, 512, lam_init)

        s0 = jnp.zeros((bsz, 2, QK_W, GLA_DV), F32)
        ocf_c, ocb_c, s_ctx = _gla(pc_, s0, lc)
        ocf, ocb, _ = _gla(pl_, s_ctx, tb)

        w_o = w_out[l].astype(BF16)
        on = onorm_c[l][None, :]
        wg, wu, wd = w_g[l].astype(BF16), w_u[l].astype(BF16), w_d[l].astype(BF16)
        cw, cb = conv_w[l], conv_b[l][None, :]
        h = _outproj(h, o_a, o_b, ocf, ocb, pl_["gc"], on, w_o, g1, tm)
        h = _ffn(h, sc2, sh2, n2, wg, wu, wd, cw, cb, g2, tm, 1408)
        if with_ctx_out:
            o_a_c = _flash(pc_["qa"], pc_["ka"], pc_["vat"], None, None, lc, lc)
            o_b_c = _flash(pc_["qb"], pc_["kb"], pc_["vbt"], None, diff, lc, lc, lam_init)
            hc = _outproj(hc, o_a_c, o_b_c, ocf_c, ocb_c, pc_["gc"], on, w_o, cg1, lc)
            hc = _ffn(hc, csc2, csh2, n2, wg, wu, wd, cw, cb, cg2, lc, 1408)
    return h
```

```python
import functools
import math

import jax
import jax.numpy as jnp
import numpy as np
from jax import lax
from jax.experimental import pallas as pl
from jax.experimental.pallas import tpu as pltpu

F32 = jnp.float32
BF16 = jnp.bfloat16

D_MODEL = 1024
GRID_W = 64
HEAD_W = 64
N_HEADS = 4
NA_WIN_ROWS = 8
NA_WIN_COLS = 16
NA_Q_ROWS = 4
NA_K_ROWS = 12
DA_QK_DIM = 32
GLA_DK = 64
GLA_DV = 128
GLA_CHUNK = 64
GLA_GATE_RANK = 16
GLA_GATE_NORM = 16.0
GATE_PAD = 128
FFN_DIM = 2816
ROPE_THETA = 10000.0
EPS = 1e-6
NEG_INF = -1e30
LOG2E = math.log2(math.e)
QK_W = N_HEADS * HEAD_W
GLA_V_W = N_HEADS * GLA_DV
VMEM_LIMIT = 56 * 1024 * 1024

_NT = (((1,), (1,)), ((), ()))


def _cparams(n_axes):
    return pltpu.CompilerParams(dimension_semantics=("arbitrary",) * n_axes,
                                vmem_limit_bytes=VMEM_LIMIT)


def _split_bf16(x, parts):
    out = []
    r = x
    for _ in range(parts):
        p = r.astype(BF16)
        out.append(p)
        r = r - p.astype(F32)
    return out


def _dot_f32_lhs(x, m_bf16, parts):
    acc = None
    for p in _split_bf16(x, parts):
        t = jnp.dot(p, m_bf16, preferred_element_type=F32)
        acc = t if acc is None else acc + t
    return acc


def _dot_f32_rhs(m_bf16, x, parts):
    acc = None
    for p in _split_bf16(x, parts):
        t = jnp.dot(m_bf16, p, preferred_element_type=F32)
        acc = t if acc is None else acc + t
    return acc


def _silu(x):
    return x * (1.0 / (1.0 + jnp.exp(-x)))


def _log_sigmoid(x):
    return jnp.minimum(x, 0.0) - jnp.log(1.0 + jnp.exp(-jnp.abs(x)))


def _ada_kernel(c_ref, w_ref, b_ref, o_ref):
    s = _silu(c_ref[...])
    w = w_ref[0]
    acc = None
    for sp in _split_bf16(s, 3):
        for wp in _split_bf16(w, 2):
            t = jnp.dot(sp, wp, preferred_element_type=F32)
            acc = t if acc is None else acc + t
    o_ref[0] = acc + b_ref[0]


def _ada(cvec, w_ada, b_ada):
    n_l, d, n6 = w_ada.shape
    tn = 1536
    return pl.pallas_call(
        _ada_kernel,
        grid=(n_l, n6 // tn),
        in_specs=[pl.BlockSpec((16, d), lambda l, j: (0, 0)),
                  pl.BlockSpec((1, d, tn), lambda l, j: (l, 0, j)),
                  pl.BlockSpec((1, 1, tn), lambda l, j: (l, 0, j))],
        out_specs=pl.BlockSpec((1, 16, tn), lambda l, j: (l, 0, j)),
        out_shape=jax.ShapeDtypeStruct((n_l, 16, n6), F32),
        compiler_params=_cparams(2),
        name="ada_proj",
    )(cvec, w_ada, b_ada.reshape(n_l, 1, n6))


def _group_rms(x, gmat, gain, gsz):
    ssq = _dot_f32_lhs(x * x, gmat, 2)
    return x * lax.rsqrt(ssq * (1.0 / gsz) + EPS) * gain


def _inproj_kernel(h_ref, sc_ref, sh_ref, n1_ref, wn_ref, wt_ref, g64_ref, g32_ref, gains_ref,
                   w2_ref, b2_ref, *rest, rope):
    if rope:
        cos_ref, s1_ref, s2_ref = rest[:3]
        rest = rest[3:]
    (qa_ref, ka_ref, qb_ref, kb_ref, qc_ref, kc_ref, vc_ref, gc_ref, gf_ref, gb_ref,
     vat_ref, vbt_ref) = rest

    x = h_ref[0]
    y = x * lax.rsqrt(jnp.mean(x * x, axis=-1, keepdims=True) + EPS) * n1_ref[...]
    xn = (y * (1.0 + sc_ref[0]) + sh_ref[0]).astype(BF16)

    def proj(lo, hi):
        return jnp.dot(xn, wn_ref[:, lo:hi], preferred_element_type=F32)

    g64 = g64_ref[...]
    g32 = g32_ref[...]
    gains = gains_ref[...]

    def rot(v):
        if not rope:
            return v
        return (v * cos_ref[...] + pltpu.roll(v, QK_W - 8, 1) * s1_ref[...]
                + pltpu.roll(v, 8, 1) * s2_ref[...])

    qa_ref[0] = (_group_rms(proj(0, 256), g64, gains[0:1], HEAD_W) * (HEAD_W ** -0.5 * LOG2E)).astype(BF16)
    ka_ref[0] = _group_rms(proj(256, 512), g64, gains[1:2], HEAD_W).astype(BF16)
    qb_ref[0] = (rot(_group_rms(proj(512, 768), g32, gains[2:3], DA_QK_DIM))
                 * (DA_QK_DIM ** -0.5 * LOG2E)).astype(BF16)
    kb_ref[0] = rot(_group_rms(proj(768, 1024), g32, gains[3:4], DA_QK_DIM)).astype(BF16)
    qc_ref[0] = proj(1024, 1280) * (GLA_DK ** -0.5)
    kc_ref[0] = proj(1280, 1536)
    vc_ref[0] = proj(1536, 2048).astype(BF16)
    gc_ref[0] = proj(2048, 2560)
    a_lr = proj(2560, 2560 + GATE_PAD)
    pre = None
    for ap in _split_bf16(a_lr, 2):
        for wp in (w2_ref[0], w2_ref[1]):
            t = jnp.dot(ap, wp, preferred_element_type=F32)
            pre = t if pre is None else pre + t
    gate = _log_sigmoid(pre + b2_ref[...]) * (1.0 / GLA_GATE_NORM)
    gf_ref[0] = gate[:, :QK_W]
    gb_ref[0] = gate[:, QK_W:]
    vt = lax.dot_general(wt_ref[...], xn, _NT, preferred_element_type=F32)
    vat_ref[0] = vt[:QK_W].astype(BF16)
    vbt_ref[0] = vt[QK_W:].astype(BF16)


def _inproj(h, sc, sh, n1, wts, rope_tabs, tm):
    b, t, d = h.shape
    nt = t // tm
    rope = rope_tabs is not None
    const = lambda shape: pl.BlockSpec(shape, lambda i, bb: (0,) * len(shape))
    tok = lambda w: pl.BlockSpec((1, tm, w), lambda i, bb: (bb, i, 0))
    mod = pl.BlockSpec((1, 1, d), lambda i, bb: (bb, 0, 0))
    in_specs = [tok(d), mod, mod, const((1, d)), const(wts["wn"].shape), const(wts["wt"].shape),
                const((QK_W, QK_W)), const((QK_W, QK_W)), const((4, QK_W)),
                const((2, GATE_PAD, 2 * QK_W)), const((1, 2 * QK_W))]
    args = [h, sc, sh, n1, wts["wn"], wts["wt"], wts["g64"], wts["g32"], wts["gains"],
            wts["w2"], wts["b2"]]
    if rope:
        in_specs += [pl.BlockSpec((tm, QK_W), lambda i, bb: (i, 0))] * 3
        args += list(rope_tabs)
    tspec = pl.BlockSpec((1, QK_W, tm), lambda i, bb: (bb, 0, i))
    out_specs = [tok(QK_W)] * 6 + [tok(GLA_V_W), tok(GLA_V_W), tok(QK_W), tok(QK_W), tspec, tspec]
    sd = jax.ShapeDtypeStruct
    out_shape = [sd((b, t, QK_W), BF16)] * 4 + [sd((b, t, QK_W), F32)] * 2 + [
        sd((b, t, GLA_V_W), BF16), sd((b, t, GLA_V_W), F32), sd((b, t, QK_W), F32),
        sd((b, t, QK_W), F32), sd((b, QK_W, t), BF16), sd((b, QK_W, t), BF16)]
    outs = pl.pallas_call(
        functools.partial(_inproj_kernel, rope=rope),
        grid=(nt, b), in_specs=in_specs, out_specs=out_specs, out_shape=out_shape,
        compiler_params=_cparams(2), name="in_proj_rope" if rope else "in_proj",
    )(*args)
    names = ("qa", "ka", "qb", "kb", "qc", "kc", "vc", "gc", "gf", "gb", "vat", "vbt")
    return dict(zip(names, outs))


def _head_mask(shape, lo, hi):
    lane = lax.broadcasted_iota(jnp.int32, shape, 1)
    return (lane >= lo) & (lane < hi)


def _na_kernel(q_ref, k_ref, vt_ref, kc_ref, vct_ref, tab_ref, o_ref, *, n_rows):
    i = pl.program_id(1)
    nq = NA_Q_ROWS * GRID_W
    nk = NA_K_ROWS * GRID_W
    base_row = jnp.clip(NA_Q_ROWS * i - NA_WIN_ROWS // 2, 0, n_rows - NA_K_ROWS)
    base = pl.multiple_of(base_row * GRID_W, 256)
    tab_off = pl.multiple_of((base_row - NA_Q_ROWS * i + NA_WIN_ROWS) * GRID_W, 256)
    kw = k_ref[0, pl.ds(base, nk), :]
    vw = vt_ref[0, :, pl.ds(base, nk)]
    kc = kc_ref[0]
    vct = vct_ref[0]
    q = q_ref[0]
    zero = jnp.zeros_like(q)
    key_row = base_row + lax.broadcasted_iota(jnp.int32, (nk, nq), 0) // GRID_W
    q_row = NA_Q_ROWS * i + lax.broadcasted_iota(jnp.int32, (nk, nq), 1) // GRID_W
    r0 = jnp.clip(q_row - NA_WIN_ROWS // 2, 0, n_rows - NA_WIN_ROWS)
    row_ok = (key_row >= r0) & (key_row < r0 + NA_WIN_ROWS)
    for h in range(N_HEADS):
        qm = jnp.where(_head_mask(q.shape, h * HEAD_W, (h + 1) * HEAD_W), q, zero)
        bias = jnp.where(row_ok, tab_ref[h, pl.ds(tab_off, nk), :], NEG_INF)
        s = lax.dot_general(kw, qm, _NT, preferred_element_type=F32) + bias
        sc = lax.dot_general(kc, qm, _NT, preferred_element_type=F32)
        m = jnp.maximum(jnp.max(s, axis=0, keepdims=True), jnp.max(sc, axis=0, keepdims=True))
        p = jnp.exp2(s - m)
        pc = jnp.exp2(sc - m)
        l = jnp.sum(p, axis=0, keepdims=True) + jnp.sum(pc, axis=0, keepdims=True)
        rows = slice(h * HEAD_W, (h + 1) * HEAD_W)
        ot = (jnp.dot(vw[rows], p.astype(BF16), preferred_element_type=F32)
              + jnp.dot(vct[rows], pc.astype(BF16), preferred_element_type=F32))
        o_ref[0, :, rows] = (ot / l).T.astype(BF16)


NA_TAB_ROWS = NA_WIN_ROWS + NA_K_ROWS
RPB_ROWS = 2 * NA_WIN_ROWS - 1
RPB_COLS = 2 * NA_WIN_COLS - 1


def _na_table_kernel(rpb_ref, o_ref):
    h = pl.program_id(0)
    c = lax.broadcasted_iota(jnp.int32, (GRID_W, GRID_W), 0)
    w = lax.broadcasted_iota(jnp.int32, (GRID_W, GRID_W), 1)
    co = jnp.clip(c - w, -(NA_WIN_COLS - 1), NA_WIN_COLS - 1) + NA_WIN_COLS - 1
    c0 = jnp.clip(w - NA_WIN_COLS // 2, 0, GRID_W - NA_WIN_COLS)
    col_ok = (c >= c0) & (c < c0 + NA_WIN_COLS)
    tiles = []
    for ro in range(RPB_ROWS):
        t = jnp.zeros((GRID_W, GRID_W), F32)
        for kk in range(RPB_COLS):
            t = jnp.where(co == kk, rpb_ref[(h * RPB_ROWS + ro) * RPB_COLS + kk], t)
        tiles.append(jnp.where(col_ok, t * LOG2E, NEG_INF))
    for u in range(NA_TAB_ROWS):
        for b in range(NA_Q_ROWS):
            ro = min(max(u - NA_WIN_ROWS - b + NA_WIN_ROWS - 1, 0), RPB_ROWS - 1)
            o_ref[0, u * GRID_W:(u + 1) * GRID_W, b * GRID_W:(b + 1) * GRID_W] = tiles[ro]


def _na_table(rpb):
    n_h = rpb.shape[0]
    flat = jnp.zeros((2048,), F32).at[:n_h * RPB_ROWS * RPB_COLS].set(rpb.reshape(-1))
    return pl.pallas_call(
        _na_table_kernel, grid=(n_h,),
        in_specs=[pl.BlockSpec(memory_space=pltpu.SMEM)],
        out_specs=pl.BlockSpec((1, NA_TAB_ROWS * GRID_W, NA_Q_ROWS * GRID_W), lambda h: (h, 0, 0)),
        out_shape=jax.ShapeDtypeStruct((n_h, NA_TAB_ROWS * GRID_W, NA_Q_ROWS * GRID_W), F32),
        compiler_params=_cparams(1), name="na_bias_table",
    )(flat)


def _na_attention(q, k, vt, kc, vct, tab):
    b, s, _ = q.shape
    rows = s // GRID_W
    nb = rows // NA_Q_ROWS
    nq = NA_Q_ROWS * GRID_W
    lc = kc.shape[1]
    return pl.pallas_call(
        functools.partial(_na_kernel, n_rows=rows),
        grid=(b, nb),
        in_specs=[pl.BlockSpec((1, nq, QK_W), lambda bb, i: (bb, i, 0)),
                  pl.BlockSpec((1, s, QK_W), lambda bb, i: (bb, 0, 0)),
                  pl.BlockSpec((1, QK_W, s), lambda bb, i: (bb, 0, 0)),
                  pl.BlockSpec((1, lc, QK_W), lambda bb, i: (bb, 0, 0)),
                  pl.BlockSpec((1, QK_W, lc), lambda bb, i: (bb, 0, 0)),
                  pl.BlockSpec(tab.shape, lambda bb, i: (0, 0, 0), pipeline_mode=pl.Buffered(1))],
        out_specs=pl.BlockSpec((1, nq, QK_W), lambda bb, i: (bb, i, 0)),
        out_shape=jax.ShapeDtypeStruct((b, s, QK_W), BF16),
        compiler_params=_cparams(2), name="na_attention",
    )(q, k, vt, kc, vct, tab)


def _flash_kernel(*refs, n_comp, tk, n_chunks, has_extra, lam_init):
    q_ref, k_ref, vt_ref = refs[:3]
    refs = refs[3:]
    if has_extra:
        kx_ref, vxt_ref = refs[:2]
        refs = refs[2:]
    if n_comp == 2:
        lam_ref, gain_ref = refs[:2]
        refs = refs[2:]
    o_ref, qm_scr, m_scr, l_scr, acc_scr = refs

    q = q_ref[0]
    tq = q.shape[0]
    dsub = HEAD_W // n_comp
    ng = N_HEADS * n_comp
    zero = jnp.zeros_like(q)
    for g in range(ng):
        qm_scr[g * tq:(g + 1) * tq, :] = jnp.where(_head_mask(q.shape, g * dsub, (g + 1) * dsub), q, zero)
    m_scr[...] = jnp.full(m_scr.shape, NEG_INF, F32)
    l_scr[...] = jnp.zeros(l_scr.shape, F32)
    acc_scr[...] = jnp.zeros(acc_scr.shape, F32)

    def chunk(kc, vtc):
        s = lax.dot_general(kc, qm_scr[...], _NT, preferred_element_type=F32)
        for g in range(ng):
            h = g // n_comp
            sg = s[:, g * tq:(g + 1) * tq]
            m_old = m_scr[g:g + 1]
            m_new = jnp.maximum(m_old, jnp.max(sg, axis=0, keepdims=True))
            alpha = jnp.exp2(m_old - m_new)
            p = jnp.exp2(sg - m_new)
            l_scr[g:g + 1] = alpha * l_scr[g:g + 1] + jnp.sum(p, axis=0, keepdims=True)
            rows = slice(g * HEAD_W, (g + 1) * HEAD_W)
            acc_scr[rows] = alpha * acc_scr[rows] + jnp.dot(
                vtc[h * HEAD_W:(h + 1) * HEAD_W], p.astype(BF16), preferred_element_type=F32)
            m_scr[g:g + 1] = m_new

    def body(j, carry):
        off = pl.multiple_of(j * tk, tk)
        chunk(k_ref[0, pl.ds(off, tk), :], vt_ref[0, :, pl.ds(off, tk)])
        return carry

    lax.fori_loop(0, n_chunks, body, 0, unroll=2)
    if has_extra:
        chunk(kx_ref[0], vxt_ref[0])

    if n_comp == 2:
        lv = lam_ref[...]
        lam = (jnp.exp(jnp.sum(lv[0:1] * lv[1:2], axis=1, keepdims=True))
               - jnp.exp(jnp.sum(lv[2:3] * lv[3:4], axis=1, keepdims=True)) + lam_init)
    for h in range(N_HEADS):
        cols = slice(h * HEAD_W, (h + 1) * HEAD_W)
        if n_comp == 2:
            g0, g1 = 2 * h, 2 * h + 1
            o = (acc_scr[g0 * HEAD_W:(g0 + 1) * HEAD_W] / l_scr[g0:g0 + 1]
                 - lam * (acc_scr[g1 * HEAD_W:(g1 + 1) * HEAD_W] / l_scr[g1:g1 + 1]))
            o = (o * lax.rsqrt(jnp.mean(o * o, axis=0, keepdims=True) + EPS)
                 * gain_ref[...] * (1.0 - lam_init))
        else:
            o = acc_scr[cols] / l_scr[h:h + 1]
        o_ref[0, :, cols] = o.T.astype(BF16)


def _flash(q, k, vt, extra, diff, tq, tk, lam_init=0.0):
    b, t, _ = q.shape
    tkeys = k.shape[1]
    n_comp = 2 if diff is not None else 1
    ng = N_HEADS * n_comp
    in_specs = [pl.BlockSpec((1, tq, QK_W), lambda bb, i: (bb, i, 0)),
                pl.BlockSpec((1, tkeys, QK_W), lambda bb, i: (bb, 0, 0)),
                pl.BlockSpec((1, QK_W, tkeys), lambda bb, i: (bb, 0, 0))]
    args = [q, k, vt]
    if extra is not None:
        lx = extra[0].shape[1]
        in_specs += [pl.BlockSpec((1, lx, QK_W), lambda bb, i: (bb, 0, 0)),
                     pl.BlockSpec((1, QK_W, lx), lambda bb, i: (bb, 0, 0))]
        args += list(extra)
    if diff is not None:
        in_specs += [pl.BlockSpec((4, DA_QK_DIM), lambda bb, i: (0, 0)),
                     pl.BlockSpec((HEAD_W, 1), lambda bb, i: (0, 0))]
        args += list(diff)
    return pl.pallas_call(
        functools.partial(_flash_kernel, n_comp=n_comp, tk=tk, n_chunks=tkeys // tk,
                          has_extra=extra is not None, lam_init=lam_init),
        grid=(b, t // tq), in_specs=in_specs,
        out_specs=pl.BlockSpec((1, tq, QK_W), lambda bb, i: (bb, i, 0)),
        out_shape=jax.ShapeDtypeStruct((b, t, QK_W), BF16),
        scratch_shapes=[pltpu.VMEM((ng * tq, QK_W), BF16), pltpu.VMEM((ng, tq), F32),
                        pltpu.VMEM((ng, tq), F32), pltpu.VMEM((ng * HEAD_W, tq), F32)],
        compiler_params=_cparams(2),
        name="diff_attention" if diff is not None else "dense_attention",
    )(*args)


def _gla_dir(q_ref, k_ref, v_ref, g_ref, o_ref, state, cum_ref, order, reverse, mask64):
    c = GLA_CHUNK
    g_all = g_ref[0]
    b_all = _dot_f32_rhs(cum_ref[...], g_all, 3)
    for ci in order:
        r0 = ci * c
        bch = b_all[r0:r0 + c]
        qch = q_ref[0, r0:r0 + c, :]
        kch = k_ref[0, r0:r0 + c, :]
        vch = v_ref[0, r0:r0 + c, :]
        btot = bch[0:1] if reverse else bch[c - 1:c]
        bmid = bch[c // 2:c // 2 + 1]
        q_in = (qch * jnp.exp(bch - bmid)).astype(BF16)
        k_in = (kch * jnp.exp(bmid - bch)).astype(BF16)
        q_st = (qch * jnp.exp(bch)).astype(BF16)
        k_st = kch * jnp.exp(btot - bch)
        k_st_t = k_st.T.astype(BF16)
        dec = jnp.broadcast_to(jnp.exp(btot), (GLA_DV, QK_W)).T
        s_old = state[...]
        s_bf = s_old.astype(BF16)
        upd = []
        for h in range(N_HEADS):
            kr = slice(h * GLA_DK, (h + 1) * GLA_DK)
            vr = slice(h * GLA_DV, (h + 1) * GLA_DV)
            attn = lax.dot_general(q_in[:, kr], k_in[:, kr], _NT, preferred_element_type=F32)
            attn = jnp.where(mask64, attn, 0.0).astype(BF16)
            o = (jnp.dot(attn, vch[:, vr], preferred_element_type=F32)
                 + jnp.dot(q_st[:, kr], s_bf[kr], preferred_element_type=F32))
            o_ref[0, r0:r0 + c, vr] = o
            upd.append(jnp.dot(k_st_t[kr], vch[:, vr], preferred_element_type=F32))
        state[...] = dec * s_old + jnp.concatenate(upd, axis=0)


def _gla_kernel(qf_ref, kf_ref, vf_ref, gf_ref, qb_ref, kb_ref, vb_ref, gb_ref, s0_ref,
                cumf_ref, cumb_ref, of_ref, ob_ref, sfin_ref, sf, sb, *, n_chunks):
    i = pl.program_id(1)

    @pl.when(i == 0)
    def _():
        sf[...] = s0_ref[0, 0]
        sb[...] = s0_ref[0, 1]

    t = lax.broadcasted_iota(jnp.int32, (GLA_CHUNK, GLA_CHUNK), 0)
    s = lax.broadcasted_iota(jnp.int32, (GLA_CHUNK, GLA_CHUNK), 1)
    _gla_dir(qf_ref, kf_ref, vf_ref, gf_ref, of_ref, sf, cumf_ref, range(n_chunks), False, s <= t)
    _gla_dir(qb_ref, kb_ref, vb_ref, gb_ref, ob_ref, sb, cumb_ref, range(n_chunks - 1, -1, -1), True, s >= t)

    @pl.when(i == pl.num_programs(1) - 1)
    def _():
        sfin_ref[0, 0] = sf[...]
        sfin_ref[0, 1] = sb[...]


def _gla(p, s0, tb):
    b, t, _ = p["qc"].shape
    nb = t // tb
    n_chunks = tb // GLA_CHUNK
    idx = np.arange(tb)
    same = (idx[:, None] // GLA_CHUNK) == (idx[None, :] // GLA_CHUNK)
    cumf = jnp.asarray(same & (idx[None, :] <= idx[:, None]), BF16)
    cumb = jnp.asarray(same & (idx[None, :] >= idx[:, None]), BF16)
    fwd = lambda w: pl.BlockSpec((1, tb, w), lambda bb, i: (bb, i, 0))
    bwd = lambda w: pl.BlockSpec((1, tb, w), lambda bb, i: (bb, nb - 1 - i, 0))
    st = pl.BlockSpec((1, 2, QK_W, GLA_DV), lambda bb, i: (bb, 0, 0, 0))
    cm = pl.BlockSpec((tb, tb), lambda bb, i: (0, 0))
    sd = jax.ShapeDtypeStruct
    return pl.pallas_call(
        functools.partial(_gla_kernel, n_chunks=n_chunks),
        grid=(b, nb),
        in_specs=[fwd(QK_W), fwd(QK_W), fwd(GLA_V_W), fwd(QK_W),
                  bwd(QK_W), bwd(QK_W), bwd(GLA_V_W), bwd(QK_W), st, cm, cm],
        out_specs=[fwd(GLA_V_W), bwd(GLA_V_W), st],
        out_shape=[sd((b, t, GLA_V_W), F32), sd((b, t, GLA_V_W), F32), sd((b, 2, QK_W, GLA_DV), F32)],
        scratch_shapes=[pltpu.VMEM((QK_W, GLA_DV), F32), pltpu.VMEM((QK_W, GLA_DV), F32)],
        compiler_params=_cparams(2), name="gla_scan",
    )(p["qc"], p["kc"], p["vc"], p["gf"], p["qc"], p["kc"], p["vc"], p["gb"], s0, cumf, cumb)


def _outproj_kernel(h_ref, oa_ref, ob_ref, ocf_ref, ocb_ref, gc_ref, on_ref, w_ref, g1_ref, o_ref):
    oc = ocf_ref[0] + ocb_ref[0]
    gate = gc_ref[0]
    parts = []
    for h in range(N_HEADS):
        vr = slice(h * GLA_DV, (h + 1) * GLA_DV)
        x = oc[:, vr]
        x = x * lax.rsqrt(jnp.mean(x * x, axis=-1, keepdims=True) + EPS) * on_ref[...]
        parts.append((x * _silu(gate[:, vr])).astype(BF16))
    y = (jnp.dot(oa_ref[0], w_ref[0:QK_W], preferred_element_type=F32)
         + jnp.dot(ob_ref[0], w_ref[QK_W:2 * QK_W], preferred_element_type=F32)
         + jnp.dot(jnp.concatenate(parts, axis=1), w_ref[2 * QK_W:], preferred_element_type=F32))
    o_ref[0] = h_ref[0] + g1_ref[0] * y


def _outproj(h, oa, ob, ocf, ocb, gc, onorm, w_out, g1, tm):
    b, t, d = h.shape
    tok = lambda w: pl.BlockSpec((1, tm, w), lambda bb, i: (bb, i, 0))
    return pl.pallas_call(
        _outproj_kernel, grid=(b, t // tm),
        in_specs=[tok(d), tok(QK_W), tok(QK_W), tok(GLA_V_W), tok(GLA_V_W), tok(GLA_V_W),
                  pl.BlockSpec((1, GLA_DV), lambda bb, i: (0, 0)),
                  pl.BlockSpec(w_out.shape, lambda bb, i: (0, 0)),
                  pl.BlockSpec((1, 1, d), lambda bb, i: (bb, 0, 0))],
        out_specs=tok(d), out_shape=jax.ShapeDtypeStruct((b, t, d), F32),
        compiler_params=_cparams(2), name="out_proj",
    )(h, oa, ob, ocf, ocb, gc, onorm, w_out, g1)


FFN_HALO = 16


def _ffn_kernel(h_ref, hp_ref, hn_ref, sc_ref, sh_ref, n2_ref, wg_ref, wu_ref, wd_ref, cw_ref, cb_ref,
                g2_ref, o_ref, xn_scr, a_scr, *, fc):
    i = pl.program_id(1)
    nt = pl.num_programs(1)
    tm = h_ref.shape[1]
    n2 = n2_ref[...]
    sc = sc_ref[0]
    sh = sh_ref[0]

    def normed(x):
        y = x * lax.rsqrt(jnp.mean(x * x, axis=-1, keepdims=True) + EPS) * n2
        return y * (1.0 + sc) + sh

    x = h_ref[0]
    prev_ok = (i > 0).astype(F32)
    next_ok = (i < nt - 1).astype(F32)
    xn_scr[0:FFN_HALO] = (normed(hp_ref[0]) * prev_ok).astype(BF16)
    xn_scr[FFN_HALO:FFN_HALO + tm] = normed(x).astype(BF16)
    xn_scr[FFN_HALO + tm:] = (normed(hn_ref[0]) * next_ok).astype(BF16)

    acc = None
    for f0 in range(0, FFN_DIM, fc):
        a_scr[...] = jnp.dot(xn_scr[...], wg_ref[:, f0:f0 + fc], preferred_element_type=F32)
        u = jnp.dot(xn_scr[FFN_HALO:FFN_HALO + tm], wu_ref[:, f0:f0 + fc], preferred_element_type=F32)
        cw = cw_ref[:, f0:f0 + fc]
        a = (cb_ref[:, f0:f0 + fc]
             + a_scr[FFN_HALO - 1:FFN_HALO - 1 + tm] * cw[0:1]
             + a_scr[FFN_HALO:FFN_HALO + tm] * cw[1:2]
             + a_scr[FFN_HALO + 1:FFN_HALO + 1 + tm] * cw[2:3])
        g = (_silu(a) * u).astype(BF16)
        t = jnp.dot(g, wd_ref[f0:f0 + fc], preferred_element_type=F32)
        acc = t if acc is None else acc + t
    o_ref[0] = x + g2_ref[0] * acc


def _ffn(h, sc, sh, n2, wg, wu, wd, cw, cb, g2, tm, fc):
    b, t, d = h.shape
    nt = t // tm
    hb = tm // FFN_HALO
    tok = pl.BlockSpec((1, tm, d), lambda bb, i: (bb, i, 0))
    prev = pl.BlockSpec((1, FFN_HALO, d), lambda bb, i: (bb, jnp.maximum(i * hb - 1, 0), 0))
    nxt = pl.BlockSpec((1, FFN_HALO, d), lambda bb, i: (bb, jnp.minimum((i + 1) * hb, nt * hb - 1), 0))
    mod = pl.BlockSpec((1, 1, d), lambda bb, i: (bb, 0, 0))
    const = lambda shape: pl.BlockSpec(shape, lambda bb, i: (0,) * len(shape),
                                       pipeline_mode=pl.Buffered(1))
    return pl.pallas_call(
        functools.partial(_ffn_kernel, fc=fc), grid=(b, nt),
        in_specs=[tok, prev, nxt, mod, mod, const((1, d)), const(wg.shape), const(wu.shape),
                  const(wd.shape), const(cw.shape), const(cb.shape), mod],
        out_specs=tok, out_shape=jax.ShapeDtypeStruct((b, t, d), F32),
        scratch_shapes=[pltpu.VMEM((tm + 2 * FFN_HALO, d), BF16),
                        pltpu.VMEM((tm + 2 * FFN_HALO, fc), F32)],
        compiler_params=_cparams(2), name="conv_ffn",
    )(h, h, h, sc, sh, n2, wg, wu, wd, cw, cb, g2)


def _rope_tables(s):
    t = np.arange(s)
    row, col = t // GRID_W, t % GRID_W
    lane = np.arange(QK_W)
    jj = lane % DA_QK_DIM
    nf = DA_QK_DIM // 4
    inv = jnp.asarray(ROPE_THETA, F32) ** (-jnp.arange(nf, dtype=F32) / nf)
    pos = jnp.where((jj // (2 * nf) == 0)[None, :], row[:, None], col[:, None]).astype(F32)
    ang = pos * inv[jj % nf][None, :]
    first = ((jj % (2 * nf)) < nf)[None, :]
    cos, sin = jnp.cos(ang), jnp.sin(ang)
    return cos, jnp.where(first, -sin, 0.0), jnp.where(first, 0.0, sin)


def _block_ones(gsz):
    g = np.arange(QK_W) // gsz
    return jnp.asarray(g[:, None] == g[None, :], BF16)


def _layer_weights(l, w_in, qn_a, kn_a, qn_b, kn_b, w_a2_f, b_a_f, w_a2_b, b_a_b):
    w = w_in[l]
    d = w.shape[0]
    wn = jnp.concatenate([w[:, 0:512], w[:, 768:1280], w[:, 1536:3104],
                          jnp.zeros((d, GATE_PAD - 2 * GLA_GATE_RANK), F32)], axis=1).astype(BF16)
    wt = jnp.concatenate([w[:, 512:768], w[:, 1280:1536]], axis=1).T.astype(BF16)
    gains = jnp.stack([jnp.tile(qn_a[l], N_HEADS), jnp.tile(kn_a[l], N_HEADS),
                       jnp.tile(qn_b[l], 2 * N_HEADS), jnp.tile(kn_b[l], 2 * N_HEADS)])
    w2 = jnp.zeros((GATE_PAD, 2 * QK_W), F32)
    w2 = w2.at[0:GLA_GATE_RANK, 0:QK_W].set(w_a2_f[l])
    w2 = w2.at[GLA_GATE_RANK:2 * GLA_GATE_RANK, QK_W:].set(w_a2_b[l])
    w2_hi = w2.astype(BF16)
    w2_lo = (w2 - w2_hi.astype(F32)).astype(BF16)
    b2 = jnp.concatenate([b_a_f[l], b_a_b[l]])[None, :]
    return {"wn": wn, "wt": wt, "g64": _block_ones(HEAD_W), "g32": _block_ones(DA_QK_DIM),
            "gains": gains, "w2": jnp.stack([w2_hi, w2_lo]), "b2": b2}


def kernel(x, c, ctx, c_ctx, norm1, norm2, w_ada, b_ada, w_in, qn_a, kn_a, rpb_a, qn_b, kn_b,
           lam_q1, lam_k1, lam_q2, lam_k2, subln_b, w_a2_f, b_a_f, w_a2_b, b_a_b, onorm_c, w_out,
           w_g, w_u, conv_w, conv_b, w_d):
    bsz, s, d = x.shape
    lc = ctx.shape[1]
    depth = w_in.shape[0]
    rows = s // GRID_W
    tm = min(512, s)
    tb = min(512, s)

    cvec = jnp.zeros((16, d), F32).at[:bsz].set(c).at[bsz].set(c_ctx)
    ada = _ada(cvec, w_ada, b_ada)
    rope = _rope_tables(s)

    h, hc = x, ctx
    for l in range(depth):
        with_ctx_out = l < depth - 1
        lam_init = 0.8 - 0.6 * math.exp(-0.3 * l)
        m = ada[l, :bsz].reshape(bsz, 1, 6, d)
        mc = jnp.broadcast_to(ada[l, bsz].reshape(1, 1, 6, d), (bsz, 1, 6, d))
        sh1, sc1, g1, sh2, sc2, g2 = (m[:, :, j] for j in range(6))
        csh1, csc1, cg1, csh2, csc2, cg2 = (mc[:, :, j] for j in range(6))
        wts = _layer_weights(l, w_in, qn_a, kn_a, qn_b, kn_b, w_a2_f, b_a_f, w_a2_b, b_a_b)
        n1 = norm1[l][None, :]
        n2 = norm2[l][None, :]

        pl_ = _inproj(h, sc1, sh1, n1, wts, rope, tm)
        pc_ = _inproj(hc, csc1, csh1, n1, wts, None, lc)

        tab = _na_table(rpb_a[l])
        o_a = _na_attention(pl_["qa"], pl_["ka"], pl_["vat"], pc_["ka"], pc_["vat"], tab)
        lamv = jnp.stack([lam_q1[l], lam_k1[l], lam_q2[l], lam_k2[l]])
        diff = (lamv, subln_b[l][:, None])
        o_b = _flash(pl_["qb"], pl_["kb"], pl_["vbt"], (pc_["kb"], pc_["vbt"]), diff, 256, 512, lam_init)

        s0 = jnp.zeros((bsz, 2, QK_W, GLA_DV), F32)
        ocf_c, ocb_c, s_ctx = _gla(pc_, s0, lc)
        ocf, ocb, _ = _gla(pl_, s_ctx, tb)

        w_o = w_out[l].astype(BF16)
        on = onorm_c[l][None, :]
        wg, wu, wd = w_g[l].astype(BF16), w_u[l].astype(BF16), w_d[l].astype(BF16)
        cw, cb = conv_w[l], conv_b[l][None, :]
        h = _outproj(h, o_a, o_b, ocf, ocb, pl_["gc"], on, w_o, g1, tm)
        h = _ffn(h, sc2, sh2, n2, wg, wu, wd, cw, cb, g2, tm, 1408)
        if with_ctx_out:
            o_a_c = _flash(pc_["qa"], pc_["ka"], pc_["vat"], None, None, lc, lc)
            o_b_c = _flash(pc_["qb"], pc_["kb"], pc_["vbt"], None, diff, lc, lc, lam_init)
            hc = _outproj(hc, o_a_c, o_b_c, ocf_c, ocb_c, pc_["gc"], on, w_o, cg1, lc)
            hc = _ffn(hc, csc2, csh2, n2, wg, wu, wd, cw, cb, cg2, lc, 1408)
    return h
```

```python
import functools
import math

import jax
import jax.numpy as jnp
import numpy as np
from jax import lax
from jax.experimental import pallas as pl
from jax.experimental.pallas import tpu as pltpu

F32 = jnp.float32
BF16 = jnp.bfloat16

D_MODEL = 1024
GRID_W = 64
HEAD_W = 64
N_HEADS = 4
NA_WIN_ROWS = 8
NA_WIN_COLS = 16
NA_Q_ROWS = 4
NA_K_ROWS = 12
DA_QK_DIM = 32
GLA_DK = 64
GLA_DV = 128
GLA_CHUNK = 64
GLA_INTRA = 128
GLA_GATE_RANK = 16
GLA_GATE_NORM = 16.0
GATE_PAD = 128
FFN_DIM = 2816
ROPE_THETA = 10000.0
EPS = 1e-6
NEG_INF = -1e30
LOG2E = math.log2(math.e)
QK_W = N_HEADS * HEAD_W
VT_ROWS = HEAD_W + 16
VT_W = N_HEADS * VT_ROWS
GLA_V_W = N_HEADS * GLA_DV
VMEM_LIMIT = 56 * 1024 * 1024

_NT = (((1,), (1,)), ((), ()))


def _cparams(n_axes):
    return pltpu.CompilerParams(dimension_semantics=("arbitrary",) * n_axes,
                                vmem_limit_bytes=VMEM_LIMIT)


def _split_bf16(x, parts):
    out = []
    r = x
    for _ in range(parts):
        p = r.astype(BF16)
        out.append(p)
        r = r - p.astype(F32)
    return out


def _dot_f32_lhs(x, m_bf16, parts):
    acc = None
    for p in _split_bf16(x, parts):
        t = jnp.dot(p, m_bf16, preferred_element_type=F32)
        acc = t if acc is None else acc + t
    return acc


def _dot_f32_rhs(m_bf16, x, parts):
    acc = None
    for p in _split_bf16(x, parts):
        t = jnp.dot(m_bf16, p, preferred_element_type=F32)
        acc = t if acc is None else acc + t
    return acc


def _silu(x):
    return x * (1.0 / (1.0 + jnp.exp(-x)))


def _log_sigmoid(x):
    return jnp.minimum(x, 0.0) - jnp.log(1.0 + jnp.exp(-jnp.abs(x)))


def _ada_kernel(c_ref, w_ref, b_ref, o_ref):
    s = _silu(c_ref[...])
    w = w_ref[0]
    acc = None
    for sp in _split_bf16(s, 3):
        for wp in _split_bf16(w, 2):
            t = jnp.dot(sp, wp, preferred_element_type=F32)
            acc = t if acc is None else acc + t
    o_ref[0] = acc + b_ref[0]


def _ada(cvec, w_ada, b_ada):
    n_l, d, n6 = w_ada.shape
    tn = 1536
    return pl.pallas_call(
        _ada_kernel,
        grid=(n_l, n6 // tn),
        in_specs=[pl.BlockSpec((16, d), lambda l, j: (0, 0)),
                  pl.BlockSpec((1, d, tn), lambda l, j: (l, 0, j)),
                  pl.BlockSpec((1, 1, tn), lambda l, j: (l, 0, j))],
        out_specs=pl.BlockSpec((1, 16, tn), lambda l, j: (l, 0, j)),
        out_shape=jax.ShapeDtypeStruct((n_l, 16, n6), F32),
        compiler_params=_cparams(2),
        name="ada_proj",
    )(cvec, w_ada, b_ada.reshape(n_l, 1, n6))


def _group_rms(x, gmat, gain, gsz):
    ssq = _dot_f32_lhs(x * x, gmat, 1)
    return x * lax.rsqrt(ssq * (1.0 / gsz) + EPS) * gain


def _inproj_kernel(h_ref, sc_ref, sh_ref, n1_ref, wn_ref, wt_ref, g64_ref, g32_ref, gains_ref,
                   w2_ref, b2_ref, *rest, rope):
    if rope:
        cos_ref, s1_ref, s2_ref = rest[:3]
        rest = rest[3:]
    (qa_ref, ka_ref, qb_ref, kb_ref, qc_ref, kc_ref, vc_ref, gc_ref, gf_ref, gb_ref,
     vat_ref, vbt_ref) = rest

    x = h_ref[0]
    y = x * lax.rsqrt(jnp.mean(x * x, axis=-1, keepdims=True) + EPS) * n1_ref[...]
    xn = (y * (1.0 + sc_ref[0]) + sh_ref[0]).astype(BF16)

    def proj(lo, hi):
        return jnp.dot(xn, wn_ref[:, lo:hi], preferred_element_type=F32)

    g64 = g64_ref[...]
    g32 = g32_ref[...]
    gains = gains_ref[...]

    def rot(v):
        if not rope:
            return v
        return (v * cos_ref[...] + pltpu.roll(v, QK_W - 8, 1) * s1_ref[...]
                + pltpu.roll(v, 8, 1) * s2_ref[...])

    qa_ref[0] = (_group_rms(proj(0, 256), g64, gains[0:1], HEAD_W) * (HEAD_W ** -0.5 * LOG2E)).astype(BF16)
    ka_ref[0] = _group_rms(proj(256, 512), g64, gains[1:2], HEAD_W).astype(BF16)
    qb_ref[0] = (rot(_group_rms(proj(512, 768), g32, gains[2:3], DA_QK_DIM))
                 * (DA_QK_DIM ** -0.5 * LOG2E)).astype(BF16)
    kb_ref[0] = rot(_group_rms(proj(768, 1024), g32, gains[3:4], DA_QK_DIM)).astype(BF16)
    qc_ref[0] = proj(1024, 1280) * (GLA_DK ** -0.5)
    kc_ref[0] = proj(1280, 1536)
    vc_ref[0] = proj(1536, 2048).astype(BF16)
    gc_ref[0] = proj(2048, 2560)
    a_lr = proj(2560, 2560 + GATE_PAD)
    a_hi, a_lo = _split_bf16(a_lr, 2)
    pre = (jnp.dot(a_hi, w2_ref[0], preferred_element_type=F32)
           + jnp.dot(a_lo, w2_ref[0], preferred_element_type=F32)
           + jnp.dot(a_hi, w2_ref[1], preferred_element_type=F32))
    gate = _log_sigmoid(pre + b2_ref[...]) * (1.0 / GLA_GATE_NORM)
    gf_ref[0] = gate[:, :QK_W]
    gb_ref[0] = gate[:, QK_W:]
    vt = lax.dot_general(wt_ref[...], xn, _NT, preferred_element_type=F32)
    ones = jnp.ones((VT_ROWS - HEAD_W, vt.shape[1]), BF16)
    for grp, ref in enumerate((vat_ref, vbt_ref)):
        for h in range(N_HEADS):
            src = (grp * N_HEADS + h) * HEAD_W
            ref[0, h * VT_ROWS:h * VT_ROWS + HEAD_W, :] = vt[src:src + HEAD_W].astype(BF16)
            ref[0, h * VT_ROWS + HEAD_W:(h + 1) * VT_ROWS, :] = ones


def _inproj(h, sc, sh, n1, wts, rope_tabs, tm):
    b, t, d = h.shape
    nt = t // tm
    rope = rope_tabs is not None
    const = lambda shape: pl.BlockSpec(shape, lambda i, bb: (0,) * len(shape))
    tok = lambda w: pl.BlockSpec((1, tm, w), lambda i, bb: (bb, i, 0))
    mod = pl.BlockSpec((1, 1, d), lambda i, bb: (bb, 0, 0))
    in_specs = [tok(d), mod, mod, const((1, d)), const(wts["wn"].shape), const(wts["wt"].shape),
                const((QK_W, QK_W)), const((QK_W, QK_W)), const((4, QK_W)),
                const((2, GATE_PAD, 2 * QK_W)), const((1, 2 * QK_W))]
    args = [h, sc, sh, n1, wts["wn"], wts["wt"], wts["g64"], wts["g32"], wts["gains"],
            wts["w2"], wts["b2"]]
    if rope:
        in_specs += [pl.BlockSpec((tm, QK_W), lambda i, bb: (i, 0))] * 3
        args += list(rope_tabs)
    tspec = pl.BlockSpec((1, VT_W, tm), lambda i, bb: (bb, 0, i))
    out_specs = [tok(QK_W)] * 6 + [tok(GLA_V_W), tok(GLA_V_W), tok(QK_W), tok(QK_W), tspec, tspec]
    sd = jax.ShapeDtypeStruct
    out_shape = [sd((b, t, QK_W), BF16)] * 4 + [sd((b, t, QK_W), F32)] * 2 + [
        sd((b, t, GLA_V_W), BF16), sd((b, t, GLA_V_W), F32), sd((b, t, QK_W), F32),
        sd((b, t, QK_W), F32), sd((b, VT_W, t), BF16), sd((b, VT_W, t), BF16)]
    outs = pl.pallas_call(
        functools.partial(_inproj_kernel, rope=rope),
        grid=(nt, b), in_specs=in_specs, out_specs=out_specs, out_shape=out_shape,
        compiler_params=_cparams(2), name="in_proj_rope" if rope else "in_proj",
    )(*args)
    names = ("qa", "ka", "qb", "kb", "qc", "kc", "vc", "gc", "gf", "gb", "vat", "vbt")
    return dict(zip(names, outs))


def _head_mask(shape, lo, hi):
    lane = lax.broadcasted_iota(jnp.int32, shape, 1)
    return (lane >= lo) & (lane < hi)


def _na_kernel(q_ref, k_ref, vt_ref, kc_ref, vct_ref, tab_ref, o_ref, qm_scr, s_scr, *, n_rows):
    i = pl.program_id(1)
    nq = NA_Q_ROWS * GRID_W
    nk = NA_K_ROWS * GRID_W
    base_row = jnp.clip(NA_Q_ROWS * i - NA_WIN_ROWS // 2, 0, n_rows - NA_K_ROWS)
    base = pl.multiple_of(base_row * GRID_W, 256)
    tab_off = pl.multiple_of((base_row - NA_Q_ROWS * i + NA_WIN_ROWS) * GRID_W, 256)
    kw = k_ref[0, pl.ds(base, nk), :]
    vw = vt_ref[0, :, pl.ds(base, nk)]
    kc = kc_ref[0]
    vct = vct_ref[0]
    q = q_ref[0]
    zero = jnp.zeros_like(q)
    for h in range(N_HEADS):
        qm_scr[h * nq:(h + 1) * nq, :] = jnp.where(_head_mask(q.shape, h * HEAD_W, (h + 1) * HEAD_W), q, zero)
    s_scr[...] = lax.dot_general(jnp.concatenate([kw, kc], axis=0), qm_scr[...], _NT,
                                 preferred_element_type=F32)
    vt_all = jnp.concatenate([vw, vct], axis=1)
    key_row = base_row + lax.broadcasted_iota(jnp.int32, (nk, nq), 0) // GRID_W
    q_row = NA_Q_ROWS * i + lax.broadcasted_iota(jnp.int32, (nk, nq), 1) // GRID_W
    r0 = jnp.clip(q_row - NA_WIN_ROWS // 2, 0, n_rows - NA_WIN_ROWS)
    row_ok = (key_row >= r0) & (key_row < r0 + NA_WIN_ROWS)
    for h in range(N_HEADS):
        cols = slice(h * nq, (h + 1) * nq)
        bias = jnp.where(row_ok, tab_ref[h, pl.ds(tab_off, nk), :], NEG_INF)
        sw = s_scr[0:nk, cols] + bias
        sc = s_scr[nk:, cols]
        m = jnp.maximum(jnp.max(sw, axis=0, keepdims=True), jnp.max(sc, axis=0, keepdims=True))
        p = jnp.concatenate([jnp.exp2(sw - m), jnp.exp2(sc - m)], axis=0).astype(BF16)
        ot = jnp.dot(vt_all[h * VT_ROWS:(h + 1) * VT_ROWS], p, preferred_element_type=F32)
        o_ref[0, :, h * HEAD_W:(h + 1) * HEAD_W] = (ot[:HEAD_W] / ot[HEAD_W:HEAD_W + 1]).T.astype(BF16)


NA_TAB_ROWS = NA_WIN_ROWS + NA_K_ROWS
RPB_ROWS = 2 * NA_WIN_ROWS - 1
RPB_COLS = 2 * NA_WIN_COLS - 1


def _na_table_kernel(rpb_ref, o_ref):
    h = pl.program_id(0)
    c = lax.broadcasted_iota(jnp.int32, (GRID_W, GRID_W), 0)
    w = lax.broadcasted_iota(jnp.int32, (GRID_W, GRID_W), 1)
    co = jnp.clip(c - w, -(NA_WIN_COLS - 1), NA_WIN_COLS - 1) + NA_WIN_COLS - 1
    c0 = jnp.clip(w - NA_WIN_COLS // 2, 0, GRID_W - NA_WIN_COLS)
    col_ok = (c >= c0) & (c < c0 + NA_WIN_COLS)
    tiles = []
    for ro in range(RPB_ROWS):
        t = jnp.zeros((GRID_W, GRID_W), F32)
        for kk in range(RPB_COLS):
            t = jnp.where(co == kk, rpb_ref[(h * RPB_ROWS + ro) * RPB_COLS + kk], t)
        tiles.append(jnp.where(col_ok, t * LOG2E, NEG_INF))
    for u in range(NA_TAB_ROWS):
        for b in range(NA_Q_ROWS):
            ro = min(max(u - NA_WIN_ROWS - b + NA_WIN_ROWS - 1, 0), RPB_ROWS - 1)
            o_ref[0, u * GRID_W:(u + 1) * GRID_W, b * GRID_W:(b + 1) * GRID_W] = tiles[ro]


def _na_table(rpb):
    n_h = rpb.shape[0]
    flat = jnp.zeros((2048,), F32).at[:n_h * RPB_ROWS * RPB_COLS].set(rpb.reshape(-1))
    return pl.pallas_call(
        _na_table_kernel, grid=(n_h,),
        in_specs=[pl.BlockSpec(memory_space=pltpu.SMEM)],
        out_specs=pl.BlockSpec((1, NA_TAB_ROWS * GRID_W, NA_Q_ROWS * GRID_W), lambda h: (h, 0, 0)),
        out_shape=jax.ShapeDtypeStruct((n_h, NA_TAB_ROWS * GRID_W, NA_Q_ROWS * GRID_W), F32),
        compiler_params=_cparams(1), name="na_bias_table",
    )(flat)


def _na_attention(q, k, vt, kc, vct, tab):
    b, s, _ = q.shape
    rows = s // GRID_W
    nb = rows // NA_Q_ROWS
    nq = NA_Q_ROWS * GRID_W
    lc = kc.shape[1]
    return pl.pallas_call(
        functools.partial(_na_kernel, n_rows=rows),
        grid=(b, nb),
        in_specs=[pl.BlockSpec((1, nq, QK_W), lambda bb, i: (bb, i, 0)),
                  pl.BlockSpec((1, s, QK_W), lambda bb, i: (bb, 0, 0)),
                  pl.BlockSpec((1, VT_W, s), lambda bb, i: (bb, 0, 0)),
                  pl.BlockSpec((1, lc, QK_W), lambda bb, i: (bb, 0, 0)),
                  pl.BlockSpec((1, VT_W, lc), lambda bb, i: (bb, 0, 0)),
                  pl.BlockSpec(tab.shape, lambda bb, i: (0, 0, 0), pipeline_mode=pl.Buffered(1))],
        out_specs=pl.BlockSpec((1, nq, QK_W), lambda bb, i: (bb, i, 0)),
        out_shape=jax.ShapeDtypeStruct((b, s, QK_W), BF16),
        scratch_shapes=[pltpu.VMEM((N_HEADS * nq, QK_W), BF16),
                        pltpu.VMEM((NA_K_ROWS * GRID_W + lc, N_HEADS * nq), F32)],
        compiler_params=_cparams(2), name="na_attention",
    )(q, k, vt, kc, vct, tab)


def _flash_kernel(*refs, n_comp, tk, n_chunks, has_extra, lam_init):
    q_ref, k_ref, vt_ref = refs[:3]
    refs = refs[3:]
    if has_extra:
        kx_ref, vxt_ref = refs[:2]
        refs = refs[2:]
    if n_comp == 2:
        lam_ref, gain_ref = refs[:2]
        refs = refs[2:]
    o_ref, qm_scr, m_scr, acc_scr, s0_scr, s1_scr = refs

    q = q_ref[0]
    tq = q.shape[0]
    dsub = HEAD_W // n_comp
    ng = N_HEADS * n_comp
    zero = jnp.zeros_like(q)
    for g in range(ng):
        qm_scr[g * tq:(g + 1) * tq, :] = jnp.where(_head_mask(q.shape, g * dsub, (g + 1) * dsub), q, zero)
    m_scr[...] = jnp.full(m_scr.shape, NEG_INF, F32)
    acc_scr[...] = jnp.zeros(acc_scr.shape, F32)

    def scores(kc):
        return lax.dot_general(kc, qm_scr[...], _NT, preferred_element_type=F32)

    def softmax_pv(read_s, vtc):
        for g in range(ng):
            h = g // n_comp
            sg = read_s(g)
            m_old = m_scr[g:g + 1]
            m_new = jnp.maximum(m_old, jnp.max(sg, axis=0, keepdims=True))
            alpha = jnp.exp2(m_old - m_new)
            p = jnp.exp2(sg - m_new)
            rows = slice(g * VT_ROWS, (g + 1) * VT_ROWS)
            acc_scr[rows] = alpha * acc_scr[rows] + jnp.dot(
                vtc[h * VT_ROWS:(h + 1) * VT_ROWS], p.astype(BF16), preferred_element_type=F32)
            m_scr[g:g + 1] = m_new

    def keys(j):
        return k_ref[0, pl.ds(pl.multiple_of(j * tk, tk), tk), :]

    def values_t(j):
        return vt_ref[0, :, pl.ds(pl.multiple_of(j * tk, tk), tk)]

    def from_scratch(ref):
        return lambda g: ref[:, g * tq:(g + 1) * tq]

    def from_value(val):
        return lambda g: val[:, g * tq:(g + 1) * tq]

    if n_chunks == 1:
        softmax_pv(from_value(scores(keys(0))), values_t(0))
    else:
        s0_scr[...] = scores(keys(0))

        def pair(j, last):
            s1_scr[...] = scores(keys(j + 1))
            softmax_pv(from_scratch(s0_scr), values_t(j))
            if not last:
                s0_scr[...] = scores(keys(j + 2))
            softmax_pv(from_scratch(s1_scr), values_t(j + 1))

        def body(i, carry):
            pair(2 * i, False)
            return carry

        lax.fori_loop(0, n_chunks // 2 - 1, body, 0)
        pair(n_chunks - 2, True)
    if has_extra:
        softmax_pv(from_value(scores(kx_ref[0])), vxt_ref[0])

    if n_comp == 2:
        lv = lam_ref[...]
        lam = (jnp.exp(jnp.sum(lv[0:1] * lv[1:2], axis=1, keepdims=True))
               - jnp.exp(jnp.sum(lv[2:3] * lv[3:4], axis=1, keepdims=True)) + lam_init)
    for h in range(N_HEADS):
        cols = slice(h * HEAD_W, (h + 1) * HEAD_W)
        if n_comp == 2:
            a0 = acc_scr[2 * h * VT_ROWS:(2 * h + 1) * VT_ROWS]
            a1 = acc_scr[(2 * h + 1) * VT_ROWS:(2 * h + 2) * VT_ROWS]
            o = (a0[:HEAD_W] / a0[HEAD_W:HEAD_W + 1]
                 - lam * (a1[:HEAD_W] / a1[HEAD_W:HEAD_W + 1]))
            o = (o * lax.rsqrt(jnp.mean(o * o, axis=0, keepdims=True) + EPS)
                 * gain_ref[...] * (1.0 - lam_init))
        else:
            a0 = acc_scr[h * VT_ROWS:(h + 1) * VT_ROWS]
            o = a0[:HEAD_W] / a0[HEAD_W:HEAD_W + 1]
        o_ref[0, :, cols] = o.T.astype(BF16)


def _flash(q, k, vt, extra, diff, tq, tk, lam_init=0.0):
    b, t, _ = q.shape
    tkeys = k.shape[1]
    n_comp = 2 if diff is not None else 1
    ng = N_HEADS * n_comp
    in_specs = [pl.BlockSpec((1, tq, QK_W), lambda bb, i: (bb, i, 0)),
                pl.BlockSpec((1, tkeys, QK_W), lambda bb, i: (bb, 0, 0)),
                pl.BlockSpec((1, VT_W, tkeys), lambda bb, i: (bb, 0, 0))]
    args = [q, k, vt]
    if extra is not None:
        lx = extra[0].shape[1]
        in_specs += [pl.BlockSpec((1, lx, QK_W), lambda bb, i: (bb, 0, 0)),
                     pl.BlockSpec((1, VT_W, lx), lambda bb, i: (bb, 0, 0))]
        args += list(extra)
    if diff is not None:
        in_specs += [pl.BlockSpec((4, DA_QK_DIM), lambda bb, i: (0, 0)),
                     pl.BlockSpec((HEAD_W, 1), lambda bb, i: (0, 0))]
        args += list(diff)
    return pl.pallas_call(
        functools.partial(_flash_kernel, n_comp=n_comp, tk=tk, n_chunks=tkeys // tk,
                          has_extra=extra is not None, lam_init=lam_init),
        grid=(b, t // tq), in_specs=in_specs,
        out_specs=pl.BlockSpec((1, tq, QK_W), lambda bb, i: (bb, i, 0)),
        out_shape=jax.ShapeDtypeStruct((b, t, QK_W), BF16),
        scratch_shapes=[pltpu.VMEM((ng * tq, QK_W), BF16), pltpu.VMEM((ng, tq), F32),
                        pltpu.VMEM((ng * VT_ROWS, tq), F32),
                        pltpu.VMEM((tk, ng * tq), F32), pltpu.VMEM((tk, ng * tq), F32)],
        compiler_params=_cparams(2),
        name="diff_attention" if diff is not None else "dense_attention",
    )(*args)


def _gla_dir(q_ref, k_ref, v_ref, g_ref, o_ref, state, cum_ref, n_chunks, reverse):
    c = GLA_CHUNK
    tb = n_chunks * c
    b_all = _dot_f32_rhs(cum_ref[...], g_ref[0], 2)
    b3 = b_all.reshape(n_chunks, c, QK_W)
    btot3 = b3[:, 0:1] if reverse else b3[:, c - 1:c]
    bmid3 = b3[:, c // 2:c // 2 + 1]
    q3 = q_ref[0].reshape(n_chunks, c, QK_W)
    k3 = k_ref[0].reshape(n_chunks, c, QK_W)
    q_in = (q3 * jnp.exp(b3 - bmid3)).reshape(tb, QK_W).astype(BF16)
    k_in = (k3 * jnp.exp(bmid3 - b3)).reshape(tb, QK_W).astype(BF16)
    q_st = (q3 * jnp.exp(b3)).reshape(tb, QK_W).astype(BF16)
    k_st = (k3 * jnp.exp(btot3 - b3)).reshape(tb, QK_W)
    dec_all = jnp.exp(btot3)
    v = v_ref[0]

    nb_i = min(GLA_INTRA, tb)
    t = lax.broadcasted_iota(jnp.int32, (nb_i, nb_i), 0)
    s = lax.broadcasted_iota(jnp.int32, (nb_i, nb_i), 1)
    keep = ((t // c) == (s // c)) & ((s >= t) if reverse else (s <= t))
    zero = jnp.zeros((nb_i, QK_W), BF16)
    hm_i = [_head_mask((nb_i, QK_W), h * GLA_DK, (h + 1) * GLA_DK) for h in range(N_HEADS)]
    for r0 in range(0, tb, nb_i):
        qb = q_in[r0:r0 + nb_i]
        kb = k_in[r0:r0 + nb_i]
        for h in range(N_HEADS):
            vr = slice(h * GLA_DV, (h + 1) * GLA_DV)
            attn = lax.dot_general(qb, jnp.where(hm_i[h], kb, zero), _NT, preferred_element_type=F32)
            attn = jnp.where(keep, attn, 0.0).astype(BF16)
            o_ref[0, r0:r0 + nb_i, vr] = jnp.dot(attn, v[r0:r0 + nb_i, vr], preferred_element_type=F32)

    qzero = jnp.zeros((c, QK_W), BF16)
    hmasks = [_head_mask((c, QK_W), h * GLA_DK, (h + 1) * GLA_DK) for h in range(N_HEADS)]
    for ci in (range(n_chunks - 1, -1, -1) if reverse else range(n_chunks)):
        r0 = ci * c
        s_old = state[...]
        qc = q_st[r0:r0 + c]
        q_stack = jnp.concatenate([jnp.where(hm, qc, qzero) for hm in hmasks], axis=0)
        o_st = jnp.dot(q_stack, s_old.astype(BF16), preferred_element_type=F32)
        k_t = k_st[r0:r0 + c].T.astype(BF16)
        upd = []
        for h in range(N_HEADS):
            vr = slice(h * GLA_DV, (h + 1) * GLA_DV)
            o_ref[0, r0:r0 + c, vr] += o_st[h * c:(h + 1) * c]
            upd.append(jnp.dot(k_t[h * GLA_DK:(h + 1) * GLA_DK], v[r0:r0 + c, vr], preferred_element_type=F32))
        dec = jnp.broadcast_to(dec_all[ci], (GLA_DV, QK_W)).T
        state[...] = dec * s_old + jnp.concatenate(upd, axis=0)


def _gla_kernel(qf_ref, kf_ref, vf_ref, gf_ref, qb_ref, kb_ref, vb_ref, gb_ref, s0_ref,
                cumf_ref, cumb_ref, of_ref, ob_ref, sfin_ref, sf, sb, *, n_chunks):
    i = pl.program_id(1)

    @pl.when(i == 0)
    def _():
        sf[...] = s0_ref[0, 0]
        sb[...] = s0_ref[0, 1]

    _gla_dir(qf_ref, kf_ref, vf_ref, gf_ref, of_ref, sf, cumf_ref, n_chunks, False)
    _gla_dir(qb_ref, kb_ref, vb_ref, gb_ref, ob_ref, sb, cumb_ref, n_chunks, True)

    @pl.when(i == pl.num_programs(1) - 1)
    def _():
        sfin_ref[0, 0] = sf[...]
        sfin_ref[0, 1] = sb[...]


def _gla(p, s0, tb):
    b, t, _ = p["qc"].shape
    nb = t // tb
    n_chunks = tb // GLA_CHUNK
    idx = np.arange(tb)
    same = (idx[:, None] // GLA_CHUNK) == (idx[None, :] // GLA_CHUNK)
    cumf = jnp.asarray(same & (idx[None, :] <= idx[:, None]), BF16)
    cumb = jnp.asarray(same & (idx[None, :] >= idx[:, None]), BF16)
    fwd = lambda w: pl.BlockSpec((1, tb, w), lambda bb, i: (bb, i, 0))
    bwd = lambda w: pl.BlockSpec((1, tb, w), lambda bb, i: (bb, nb - 1 - i, 0))
    st = pl.BlockSpec((1, 2, QK_W, GLA_DV), lambda bb, i: (bb, 0, 0, 0))
    cm = pl.BlockSpec((tb, tb), lambda bb, i: (0, 0))
    sd = jax.ShapeDtypeStruct
    return pl.pallas_call(
        functools.partial(_gla_kernel, n_chunks=n_chunks),
        grid=(b, nb),
        in_specs=[fwd(QK_W), fwd(QK_W), fwd(GLA_V_W), fwd(QK_W),
                  bwd(QK_W), bwd(QK_W), bwd(GLA_V_W), bwd(QK_W), st, cm, cm],
        out_specs=[fwd(GLA_V_W), bwd(GLA_V_W), st],
        out_shape=[sd((b, t, GLA_V_W), F32), sd((b, t, GLA_V_W), F32), sd((b, 2, QK_W, GLA_DV), F32)],
        scratch_shapes=[pltpu.VMEM((QK_W, GLA_DV), F32), pltpu.VMEM((QK_W, GLA_DV), F32)],
        compiler_params=_cparams(2), name="gla_scan",
    )(p["qc"], p["kc"], p["vc"], p["gf"], p["qc"], p["kc"], p["vc"], p["gb"], s0, cumf, cumb)


def _outproj_kernel(h_ref, oa_ref, ob_ref, ocf_ref, ocb_ref, gc_ref, on_ref, w_ref, g1_ref, o_ref):
    oc = ocf_ref[0] + ocb_ref[0]
    gate = gc_ref[0]
    parts = []
    for h in range(N_HEADS):
        vr = slice(h * GLA_DV, (h + 1) * GLA_DV)
        x = oc[:, vr]
        x = x * lax.rsqrt(jnp.mean(x * x, axis=-1, keepdims=True) + EPS) * on_ref[...]
        parts.append((x * _silu(gate[:, vr])).astype(BF16))
    y = (jnp.dot(oa_ref[0], w_ref[0:QK_W], preferred_element_type=F32)
         + jnp.dot(ob_ref[0], w_ref[QK_W:2 * QK_W], preferred_element_type=F32)
         + jnp.dot(jnp.concatenate(parts, axis=1), w_ref[2 * QK_W:], preferred_element_type=F32))
    o_ref[0] = h_ref[0] + g1_ref[0] * y


def _outproj(h, oa, ob, ocf, ocb, gc, onorm, w_out, g1, tm):
    b, t, d = h.shape
    tok = lambda w: pl.BlockSpec((1, tm, w), lambda bb, i: (bb, i, 0))
    return pl.pallas_call(
        _outproj_kernel, grid=(b, t // tm),
        in_specs=[tok(d), tok(QK_W), tok(QK_W), tok(GLA_V_W), tok(GLA_V_W), tok(GLA_V_W),
                  pl.BlockSpec((1, GLA_DV), lambda bb, i: (0, 0)),
                  pl.BlockSpec(w_out.shape, lambda bb, i: (0, 0)),
                  pl.BlockSpec((1, 1, d), lambda bb, i: (bb, 0, 0))],
        out_specs=tok(d), out_shape=jax.ShapeDtypeStruct((b, t, d), F32),
        compiler_params=_cparams(2), name="out_proj",
    )(h, oa, ob, ocf, ocb, gc, onorm, w_out, g1)


FFN_HALO = 16


def _ffn_kernel(h_ref, hp_ref, hn_ref, sc_ref, sh_ref, n2_ref, wg_ref, wu_ref, wd_ref, cw_ref, cb_ref,
                g2_ref, o_ref, xn_scr, a_scr, *, fc):
    i = pl.program_id(1)
    nt = pl.num_programs(1)
    tm = h_ref.shape[1]
    n2 = n2_ref[...]
    sc = sc_ref[0]
    sh = sh_ref[0]

    def normed(x):
        y = x * lax.rsqrt(jnp.mean(x * x, axis=-1, keepdims=True) + EPS) * n2
        return y * (1.0 + sc) + sh

    x = h_ref[0]
    prev_ok = (i > 0).astype(F32)
    next_ok = (i < nt - 1).astype(F32)
    xn_scr[0:FFN_HALO] = (normed(hp_ref[0]) * prev_ok).astype(BF16)
    xn_scr[FFN_HALO:FFN_HALO + tm] = normed(x).astype(BF16)
    xn_scr[FFN_HALO + tm:] = (normed(hn_ref[0]) * next_ok).astype(BF16)

    acc = None
    for f0 in range(0, FFN_DIM, fc):
        a_scr[...] = jnp.dot(xn_scr[...], wg_ref[:, f0:f0 + fc], preferred_element_type=F32)
        u = jnp.dot(xn_scr[FFN_HALO:FFN_HALO + tm], wu_ref[:, f0:f0 + fc], preferred_element_type=F32)
        cw = cw_ref[:, f0:f0 + fc]
        a = (cb_ref[:, f0:f0 + fc]
             + a_scr[FFN_HALO - 1:FFN_HALO - 1 + tm] * cw[0:1]
             + a_scr[FFN_HALO:FFN_HALO + tm] * cw[1:2]
             + a_scr[FFN_HALO + 1:FFN_HALO + 1 + tm] * cw[2:3])
        g = (_silu(a) * u).astype(BF16)
        t = jnp.dot(g, wd_ref[f0:f0 + fc], preferred_element_type=F32)
        acc = t if acc is None else acc + t
    o_ref[0] = x + g2_ref[0] * acc


def _ffn(h, sc, sh, n2, wg, wu, wd, cw, cb, g2, tm, fc):
    b, t, d = h.shape
    nt = t // tm
    hb = tm // FFN_HALO
    tok = pl.BlockSpec((1, tm, d), lambda bb, i: (bb, i, 0))
    prev = pl.BlockSpec((1, FFN_HALO, d), lambda bb, i: (bb, jnp.maximum(i * hb - 1, 0), 0))
    nxt = pl.BlockSpec((1, FFN_HALO, d), lambda bb, i: (bb, jnp.minimum((i + 1) * hb, nt * hb - 1), 0))
    mod = pl.BlockSpec((1, 1, d), lambda bb, i: (bb, 0, 0))
    const = lambda shape: pl.BlockSpec(shape, lambda bb, i: (0,) * len(shape),
                                       pipeline_mode=pl.Buffered(1))
    return pl.pallas_call(
        functools.partial(_ffn_kernel, fc=fc), grid=(b, nt),
        in_specs=[tok, prev, nxt, mod, mod, const((1, d)), const(wg.shape), const(wu.shape),
                  const(wd.shape), const(cw.shape), const(cb.shape), mod],
        out_specs=tok, out_shape=jax.ShapeDtypeStruct((b, t, d), F32),
        scratch_shapes=[pltpu.VMEM((tm + 2 * FFN_HALO, d), BF16),
                        pltpu.VMEM((tm + 2 * FFN_HALO, fc), F32)],
        compiler_params=_cparams(2), name="conv_ffn",
    )(h, h, h, sc, sh, n2, wg, wu, wd, cw, cb, g2)


def _rope_tables(s):
    t = np.arange(s)
    row, col = t // GRID_W, t % GRID_W
    lane = np.arange(QK_W)
    jj = lane % DA_QK_DIM
    nf = DA_QK_DIM // 4
    inv = jnp.asarray(ROPE_THETA, F32) ** (-jnp.arange(nf, dtype=F32) / nf)
    pos = jnp.where((jj // (2 * nf) == 0)[None, :], row[:, None], col[:, None]).astype(F32)
    ang = pos * inv[jj % nf][None, :]
    first = ((jj % (2 * nf)) < nf)[None, :]
    cos, sin = jnp.cos(ang), jnp.sin(ang)
    return cos, jnp.where(first, -sin, 0.0), jnp.where(first, 0.0, sin)


def _block_ones(gsz):
    g = np.arange(QK_W) // gsz
    return jnp.asarray(g[:, None] == g[None, :], BF16)


def _layer_weights(l, w_in, qn_a, kn_a, qn_b, kn_b, w_a2_f, b_a_f, w_a2_b, b_a_b):
    w = w_in[l]
    d = w.shape[0]
    wn = jnp.concatenate([w[:, 0:512], w[:, 768:1280], w[:, 1536:3104],
                          jnp.zeros((d, GATE_PAD - 2 * GLA_GATE_RANK), F32)], axis=1).astype(BF16)
    wt = jnp.concatenate([w[:, 512:768], w[:, 1280:1536]], axis=1).T.astype(BF16)
    gains = jnp.stack([jnp.tile(qn_a[l], N_HEADS), jnp.tile(kn_a[l], N_HEADS),
                       jnp.tile(qn_b[l], 2 * N_HEADS), jnp.tile(kn_b[l], 2 * N_HEADS)])
    w2 = jnp.zeros((GATE_PAD, 2 * QK_W), F32)
    w2 = w2.at[0:GLA_GATE_RANK, 0:QK_W].set(w_a2_f[l])
    w2 = w2.at[GLA_GATE_RANK:2 * GLA_GATE_RANK, QK_W:].set(w_a2_b[l])
    w2_hi = w2.astype(BF16)
    w2_lo = (w2 - w2_hi.astype(F32)).astype(BF16)
    b2 = jnp.concatenate([b_a_f[l], b_a_b[l]])[None, :]
    return {"wn": wn, "wt": wt, "g64": _block_ones(HEAD_W), "g32": _block_ones(DA_QK_DIM),
            "gains": gains, "w2": jnp.stack([w2_hi, w2_lo]), "b2": b2}


def kernel(x, c, ctx, c_ctx, norm1, norm2, w_ada, b_ada, w_in, qn_a, kn_a, rpb_a, qn_b, kn_b,
           lam_q1, lam_k1, lam_q2, lam_k2, subln_b, w_a2_f, b_a_f, w_a2_b, b_a_b, onorm_c, w_out,
           w_g, w_u, conv_w, conv_b, w_d):
    bsz, s, d = x.shape
    lc = ctx.shape[1]
    depth = w_in.shape[0]
    rows = s // GRID_W
    tm = min(512, s)
    tb = min(512, s)

    cvec = jnp.zeros((16, d), F32).at[:bsz].set(c).at[bsz].set(c_ctx)
    ada = _ada(cvec, w_ada, b_ada)
    rope = _rope_tables(s)

    h, hc = x, ctx
    for l in range(depth):
        with_ctx_out = l < depth - 1
        lam_init = 0.8 - 0.6 * math.exp(-0.3 * l)
        m = ada[l, :bsz].reshape(bsz, 1, 6, d)
        mc = jnp.broadcast_to(ada[l, bsz].reshape(1, 1, 6, d), (bsz, 1, 6, d))
        sh1, sc1, g1, sh2, sc2, g2 = (m[:, :, j] for j in range(6))
        csh1, csc1, cg1, csh2, csc2, cg2 = (mc[:, :, j] for j in range(6))
        wts = _layer_weights(l, w_in, qn_a, kn_a, qn_b, kn_b, w_a2_f, b_a_f, w_a2_b, b_a_b)
        n1 = norm1[l][None, :]
        n2 = norm2[l][None, :]

        pl_ = _inproj(h, sc1, sh1, n1, wts, rope, tm)
        pc_ = _inproj(hc, csc1, csh1, n1, wts, None, lc)

        tab = _na_table(rpb_a[l])
        o_a = _na_attention(pl_["qa"], pl_["ka"], pl_["vat"], pc_["ka"], pc_["vat"], tab)
        lamv = jnp.stack([lam_q1[l], lam_k1[l], lam_q2[l], lam_k2[l]])
        diff = (lamv, subln_b[l][:, None])
        o_b = _flash(pl_["qb"], pl_["kb"], pl_["vbt"], (pc_["kb"], pc_["vbt"]), diff, 256, 512, lam_init)

        s0 = jnp.zeros((bsz, 2, QK_W, GLA_DV), F32)
        ocf_c, ocb_c, s_ctx = _gla(pc_, s0, lc)
        ocf, ocb, _ = _gla(pl_, s_ctx, tb)

        w_o = w_out[l].astype(BF16)
        on = onorm_c[l][None, :]
        wg, wu, wd = w_g[l].astype(BF16), w_u[l].astype(BF16), w_d[l].astype(BF16)
        cw, cb = conv_w[l], conv_b[l][None, :]
        h = _outproj(h, o_a, o_b, ocf, ocb, pl_["gc"], on, w_o, g1, tm)
        h = _ffn(h, sc2, sh2, n2, wg, wu, wd, cw, cb, g2, tm, 1408)
        if with_ctx_out:
            o_a_c = _flash(pc_["qa"], pc_["ka"], pc_["vat"], None, None, lc, lc)
            o_b_c = _flash(pc_["qb"], pc_["kb"], pc_["vbt"], None, diff, lc, lc, lam_init)
            hc = _outproj(hc, o_a_c, o_b_c, ocf_c, ocb_c, pc_["gc"], on, w_o, cg1, lc)
            hc = _ffn(hc, csc2, csh2, n2, wg, wu, wd, cw, cb, cg2, lc, 1408)
    return h
```

```python
import functools
import math

import jax
import jax.numpy as jnp
import numpy as np
from jax import lax
from jax.experimental import pallas as pl
from jax.experimental.pallas import tpu as pltpu

F32 = jnp.float32
BF16 = jnp.bfloat16

D_MODEL = 1024
GRID_W = 64
HEAD_W = 64
N_HEADS = 4
NA_WIN_ROWS = 8
NA_WIN_COLS = 16
NA_Q_ROWS = 4
NA_K_ROWS = 12
DA_QK_DIM = 32
GLA_DK = 64
GLA_DV = 128
GLA_CHUNK = 64
GLA_INTRA = 128
GLA_GATE_RANK = 16
GLA_GATE_NORM = 16.0
GATE_PAD = 128
FFN_DIM = 2816
ROPE_THETA = 10000.0
EPS = 1e-6
NEG_INF = -1e30
LOG2E = math.log2(math.e)
QK_W = N_HEADS * HEAD_W
VT_ROWS = HEAD_W + 16
VT_W = N_HEADS * VT_ROWS
GLA_V_W = N_HEADS * GLA_DV
VMEM_LIMIT = 56 * 1024 * 1024

_NT = (((1,), (1,)), ((), ()))


def _cparams(n_axes):
    return pltpu.CompilerParams(dimension_semantics=("arbitrary",) * n_axes,
                                vmem_limit_bytes=VMEM_LIMIT)


def _split_bf16(x, parts):
    out = []
    r = x
    for _ in range(parts):
        p = r.astype(BF16)
        out.append(p)
        r = r - p.astype(F32)
    return out


def _dot_f32_lhs(x, m_bf16, parts):
    acc = None
    for p in _split_bf16(x, parts):
        t = jnp.dot(p, m_bf16, preferred_element_type=F32)
        acc = t if acc is None else acc + t
    return acc


def _dot_f32_rhs(m_bf16, x, parts):
    acc = None
    for p in _split_bf16(x, parts):
        t = jnp.dot(m_bf16, p, preferred_element_type=F32)
        acc = t if acc is None else acc + t
    return acc


def _silu(x):
    return x * (1.0 / (1.0 + jnp.exp(-x)))


def _log_sigmoid(x):
    return jnp.minimum(x, 0.0) - jnp.log(1.0 + jnp.exp(-jnp.abs(x)))


def _ada_kernel(c_ref, w_ref, b_ref, o_ref):
    s = _silu(c_ref[...])
    w = w_ref[0]
    acc = None
    for sp in _split_bf16(s, 3):
        for wp in _split_bf16(w, 2):
            t = jnp.dot(sp, wp, preferred_element_type=F32)
            acc = t if acc is None else acc + t
    o_ref[0] = acc + b_ref[0]


def _ada(cvec, w_ada, b_ada):
    n_l, d, n6 = w_ada.shape
    tn = 1536
    return pl.pallas_call(
        _ada_kernel,
        grid=(n_l, n6 // tn),
        in_specs=[pl.BlockSpec((16, d), lambda l, j: (0, 0)),
                  pl.BlockSpec((1, d, tn), lambda l, j: (l, 0, j)),
                  pl.BlockSpec((1, 1, tn), lambda l, j: (l, 0, j))],
        out_specs=pl.BlockSpec((1, 16, tn), lambda l, j: (l, 0, j)),
        out_shape=jax.ShapeDtypeStruct((n_l, 16, n6), F32),
        compiler_params=_cparams(2),
        name="ada_proj",
    )(cvec, w_ada, b_ada.reshape(n_l, 1, n6))


def _group_rms(x, gmat, gain, gsz):
    ssq = _dot_f32_lhs(x * x, gmat, 1)
    return x * lax.rsqrt(ssq * (1.0 / gsz) + EPS) * gain


def _inproj_kernel(h_ref, sc_ref, sh_ref, n1_ref, wn_ref, wt_ref, g64_ref, g32_ref, gains_ref,
                   w2_ref, b2_ref, *rest, rope):
    if rope:
        cos_ref, s1_ref, s2_ref = rest[:3]
        rest = rest[3:]
    (qa_ref, ka_ref, qb_ref, kb_ref, qc_ref, kc_ref, vc_ref, gc_ref, gf_ref, gb_ref,
     vat_ref, vbt_ref, y_scr) = rest

    x = h_ref[0]
    y = x * lax.rsqrt(jnp.mean(x * x, axis=-1, keepdims=True) + EPS) * n1_ref[...]
    xn = (y * (1.0 + sc_ref[0]) + sh_ref[0]).astype(BF16)

    y_scr[...] = jnp.dot(xn, wn_ref[...], preferred_element_type=F32)
    vt = lax.dot_general(wt_ref[...], xn, _NT, preferred_element_type=F32)

    def proj(lo, hi):
        return y_scr[:, lo:hi]

    g64 = g64_ref[...]
    g32 = g32_ref[...]
    gains = gains_ref[...]

    def rot(v):
        if not rope:
            return v
        return (v * cos_ref[...] + pltpu.roll(v, QK_W - 8, 1) * s1_ref[...]
                + pltpu.roll(v, 8, 1) * s2_ref[...])

    qa_ref[0] = (_group_rms(proj(0, 256), g64, gains[0:1], HEAD_W) * (HEAD_W ** -0.5 * LOG2E)).astype(BF16)
    ka_ref[0] = _group_rms(proj(256, 512), g64, gains[1:2], HEAD_W).astype(BF16)
    qb_ref[0] = (rot(_group_rms(proj(512, 768), g32, gains[2:3], DA_QK_DIM))
                 * (DA_QK_DIM ** -0.5 * LOG2E)).astype(BF16)
    kb_ref[0] = rot(_group_rms(proj(768, 1024), g32, gains[3:4], DA_QK_DIM)).astype(BF16)
    qc_ref[0] = proj(1024, 1280) * (GLA_DK ** -0.5)
    kc_ref[0] = proj(1280, 1536)
    vc_ref[0] = proj(1536, 2048).astype(BF16)
    gc_ref[0] = proj(2048, 2560)
    a_lr = proj(2560, 2560 + GATE_PAD)
    a_hi, a_lo = _split_bf16(a_lr, 2)
    pre = (jnp.dot(a_hi, w2_ref[0], preferred_element_type=F32)
           + jnp.dot(a_lo, w2_ref[0], preferred_element_type=F32)
           + jnp.dot(a_hi, w2_ref[1], preferred_element_type=F32))
    gate = _log_sigmoid(pre + b2_ref[...]) * (1.0 / GLA_GATE_NORM)
    gf_ref[0] = gate[:, :QK_W]
    gb_ref[0] = gate[:, QK_W:]
    ones = jnp.ones((VT_ROWS - HEAD_W, vt.shape[1]), BF16)
    for grp, ref in enumerate((vat_ref, vbt_ref)):
        for h in range(N_HEADS):
            src = (grp * N_HEADS + h) * HEAD_W
            ref[0, h * VT_ROWS:h * VT_ROWS + HEAD_W, :] = vt[src:src + HEAD_W].astype(BF16)
            ref[0, h * VT_ROWS + HEAD_W:(h + 1) * VT_ROWS, :] = ones


def _inproj(h, sc, sh, n1, wts, rope_tabs, tm):
    b, t, d = h.shape
    nt = t // tm
    rope = rope_tabs is not None
    const = lambda shape: pl.BlockSpec(shape, lambda i, bb: (0,) * len(shape))
    tok = lambda w: pl.BlockSpec((1, tm, w), lambda i, bb: (bb, i, 0))
    mod = pl.BlockSpec((1, 1, d), lambda i, bb: (bb, 0, 0))
    in_specs = [tok(d), mod, mod, const((1, d)), const(wts["wn"].shape), const(wts["wt"].shape),
                const((QK_W, QK_W)), const((QK_W, QK_W)), const((4, QK_W)),
                const((2, GATE_PAD, 2 * QK_W)), const((1, 2 * QK_W))]
    args = [h, sc, sh, n1, wts["wn"], wts["wt"], wts["g64"], wts["g32"], wts["gains"],
            wts["w2"], wts["b2"]]
    if rope:
        in_specs += [pl.BlockSpec((tm, QK_W), lambda i, bb: (i, 0))] * 3
        args += list(rope_tabs)
    tspec = pl.BlockSpec((1, VT_W, tm), lambda i, bb: (bb, 0, i))
    out_specs = [tok(QK_W)] * 6 + [tok(GLA_V_W), tok(GLA_V_W), tok(QK_W), tok(QK_W), tspec, tspec]
    sd = jax.ShapeDtypeStruct
    out_shape = [sd((b, t, QK_W), BF16)] * 4 + [sd((b, t, QK_W), F32)] * 2 + [
        sd((b, t, GLA_V_W), BF16), sd((b, t, GLA_V_W), F32), sd((b, t, QK_W), F32),
        sd((b, t, QK_W), F32), sd((b, VT_W, t), BF16), sd((b, VT_W, t), BF16)]
    outs = pl.pallas_call(
        functools.partial(_inproj_kernel, rope=rope),
        grid=(nt, b), in_specs=in_specs, out_specs=out_specs, out_shape=out_shape,
        scratch_shapes=[pltpu.VMEM((tm, wts["wn"].shape[1]), F32)],
        compiler_params=_cparams(2), name="in_proj_rope" if rope else "in_proj",
    )(*args)
    names = ("qa", "ka", "qb", "kb", "qc", "kc", "vc", "gc", "gf", "gb", "vat", "vbt")
    return dict(zip(names, outs))


def _head_mask(shape, lo, hi):
    lane = lax.broadcasted_iota(jnp.int32, shape, 1)
    return (lane >= lo) & (lane < hi)


def _na_kernel(q_ref, k_ref, vt_ref, kc_ref, vct_ref, tab_ref, o_ref, qm_scr, s_scr, *, n_rows):
    i = pl.program_id(1)
    nq = NA_Q_ROWS * GRID_W
    nk = NA_K_ROWS * GRID_W
    base_row = jnp.clip(NA_Q_ROWS * i - NA_WIN_ROWS // 2, 0, n_rows - NA_K_ROWS)
    base = pl.multiple_of(base_row * GRID_W, 256)
    tab_off = pl.multiple_of((base_row - NA_Q_ROWS * i + NA_WIN_ROWS) * GRID_W, 256)
    kw = k_ref[0, pl.ds(base, nk), :]
    vw = vt_ref[0, :, pl.ds(base, nk)]
    kc = kc_ref[0]
    vct = vct_ref[0]
    q = q_ref[0]
    zero = jnp.zeros_like(q)
    for h in range(N_HEADS):
        qm_scr[h * nq:(h + 1) * nq, :] = jnp.where(_head_mask(q.shape, h * HEAD_W, (h + 1) * HEAD_W), q, zero)
    s_scr[...] = lax.dot_general(jnp.concatenate([kw, kc], axis=0), qm_scr[...], _NT,
                                 preferred_element_type=F32)
    vt_all = jnp.concatenate([vw, vct], axis=1)
    key_row = base_row + lax.broadcasted_iota(jnp.int32, (nk, nq), 0) // GRID_W
    q_row = NA_Q_ROWS * i + lax.broadcasted_iota(jnp.int32, (nk, nq), 1) // GRID_W
    r0 = jnp.clip(q_row - NA_WIN_ROWS // 2, 0, n_rows - NA_WIN_ROWS)
    row_ok = (key_row >= r0) & (key_row < r0 + NA_WIN_ROWS)
    for h in range(N_HEADS):
        cols = slice(h * nq, (h + 1) * nq)
        bias = jnp.where(row_ok, tab_ref[h, pl.ds(tab_off, nk), :], NEG_INF)
        sw = s_scr[0:nk, cols] + bias
        sc = s_scr[nk:, cols]
        m = jnp.maximum(jnp.max(sw, axis=0, keepdims=True), jnp.max(sc, axis=0, keepdims=True))
        p = jnp.concatenate([jnp.exp2(sw - m), jnp.exp2(sc - m)], axis=0).astype(BF16)
        ot = jnp.dot(vt_all[h * VT_ROWS:(h + 1) * VT_ROWS], p, preferred_element_type=F32)
        o_ref[0, :, h * HEAD_W:(h + 1) * HEAD_W] = (ot[:HEAD_W] / ot[HEAD_W:HEAD_W + 1]).T.astype(BF16)


NA_TAB_ROWS = NA_WIN_ROWS + NA_K_ROWS
RPB_ROWS = 2 * NA_WIN_ROWS - 1
RPB_COLS = 2 * NA_WIN_COLS - 1


def _na_table_kernel(rpb_ref, o_ref):
    h = pl.program_id(0)
    c = lax.broadcasted_iota(jnp.int32, (GRID_W, GRID_W), 0)
    w = lax.broadcasted_iota(jnp.int32, (GRID_W, GRID_W), 1)
    co = jnp.clip(c - w, -(NA_WIN_COLS - 1), NA_WIN_COLS - 1) + NA_WIN_COLS - 1
    c0 = jnp.clip(w - NA_WIN_COLS // 2, 0, GRID_W - NA_WIN_COLS)
    col_ok = (c >= c0) & (c < c0 + NA_WIN_COLS)
    tiles = []
    for ro in range(RPB_ROWS):
        t = jnp.zeros((GRID_W, GRID_W), F32)
        for kk in range(RPB_COLS):
            t = jnp.where(co == kk, rpb_ref[(h * RPB_ROWS + ro) * RPB_COLS + kk], t)
        tiles.append(jnp.where(col_ok, t * LOG2E, NEG_INF))
    for u in range(NA_TAB_ROWS):
        for b in range(NA_Q_ROWS):
            ro = min(max(u - NA_WIN_ROWS - b + NA_WIN_ROWS - 1, 0), RPB_ROWS - 1)
            o_ref[0, u * GRID_W:(u + 1) * GRID_W, b * GRID_W:(b + 1) * GRID_W] = tiles[ro]


def _na_table(rpb):
    n_h = rpb.shape[0]
    flat = jnp.zeros((2048,), F32).at[:n_h * RPB_ROWS * RPB_COLS].set(rpb.reshape(-1))
    return pl.pallas_call(
        _na_table_kernel, grid=(n_h,),
        in_specs=[pl.BlockSpec(memory_space=pltpu.SMEM)],
        out_specs=pl.BlockSpec((1, NA_TAB_ROWS * GRID_W, NA_Q_ROWS * GRID_W), lambda h: (h, 0, 0)),
        out_shape=jax.ShapeDtypeStruct((n_h, NA_TAB_ROWS * GRID_W, NA_Q_ROWS * GRID_W), F32),
        compiler_params=_cparams(1), name="na_bias_table",
    )(flat)


def _na_attention(q, k, vt, kc, vct, tab):
    b, s, _ = q.shape
    rows = s // GRID_W
    nb = rows // NA_Q_ROWS
    nq = NA_Q_ROWS * GRID_W
    lc = kc.shape[1]
    return pl.pallas_call(
        functools.partial(_na_kernel, n_rows=rows),
        grid=(b, nb),
        in_specs=[pl.BlockSpec((1, nq, QK_W), lambda bb, i: (bb, i, 0)),
                  pl.BlockSpec((1, s, QK_W), lambda bb, i: (bb, 0, 0)),
                  pl.BlockSpec((1, VT_W, s), lambda bb, i: (bb, 0, 0)),
                  pl.BlockSpec((1, lc, QK_W), lambda bb, i: (bb, 0, 0)),
                  pl.BlockSpec((1, VT_W, lc), lambda bb, i: (bb, 0, 0)),
                  pl.BlockSpec(tab.shape, lambda bb, i: (0, 0, 0), pipeline_mode=pl.Buffered(1))],
        out_specs=pl.BlockSpec((1, nq, QK_W), lambda bb, i: (bb, i, 0)),
        out_shape=jax.ShapeDtypeStruct((b, s, QK_W), BF16),
        scratch_shapes=[pltpu.VMEM((N_HEADS * nq, QK_W), BF16),
                        pltpu.VMEM((NA_K_ROWS * GRID_W + lc, N_HEADS * nq), F32)],
        compiler_params=_cparams(2), name="na_attention",
    )(q, k, vt, kc, vct, tab)


def _flash_kernel(*refs, n_comp, tk, n_chunks, has_extra, lam_init):
    q_ref, k_ref, vt_ref = refs[:3]
    refs = refs[3:]
    if has_extra:
        kx_ref, vxt_ref = refs[:2]
        refs = refs[2:]
    if n_comp == 2:
        lam_ref, gain_ref = refs[:2]
        refs = refs[2:]
    o_ref, qm_scr, m_scr, acc_scr, s0_scr, s1_scr = refs

    q = q_ref[0]
    tq = q.shape[0]
    dsub = HEAD_W // n_comp
    ng = N_HEADS * n_comp
    zero = jnp.zeros_like(q)
    for g in range(ng):
        qm_scr[g * tq:(g + 1) * tq, :] = jnp.where(_head_mask(q.shape, g * dsub, (g + 1) * dsub), q, zero)
    m_scr[...] = jnp.full(m_scr.shape, NEG_INF, F32)
    acc_scr[...] = jnp.zeros(acc_scr.shape, F32)

    def scores(kc):
        return lax.dot_general(kc, qm_scr[...], _NT, preferred_element_type=F32)

    def softmax_pv(read_s, vtc):
        for g in range(ng):
            h = g // n_comp
            sg = read_s(g)
            m_old = m_scr[g:g + 1]
            m_new = jnp.maximum(m_old, jnp.max(sg, axis=0, keepdims=True))
            alpha = jnp.exp2(m_old - m_new)
            p = jnp.exp2(sg - m_new)
            rows = slice(g * VT_ROWS, (g + 1) * VT_ROWS)
            acc_scr[rows] = alpha * acc_scr[rows] + jnp.dot(
                vtc[h * VT_ROWS:(h + 1) * VT_ROWS], p.astype(BF16), preferred_element_type=F32)
            m_scr[g:g + 1] = m_new

    def keys(j):
        return k_ref[0, pl.ds(pl.multiple_of(j * tk, tk), tk), :]

    def values_t(j):
        return vt_ref[0, :, pl.ds(pl.multiple_of(j * tk, tk), tk)]

    def from_scratch(ref):
        return lambda g: ref[:, g * tq:(g + 1) * tq]

    def from_value(val):
        return lambda g: val[:, g * tq:(g + 1) * tq]

    if n_chunks == 1:
        softmax_pv(from_value(scores(keys(0))), values_t(0))
    else:
        s0_scr[...] = scores(keys(0))

        def pair(j, last):
            s1_scr[...] = scores(keys(j + 1))
            softmax_pv(from_scratch(s0_scr), values_t(j))
            if not last:
                s0_scr[...] = scores(keys(j + 2))
            elif has_extra:
                s0_scr[0:kx_ref.shape[1]] = scores(kx_ref[0])
            softmax_pv(from_scratch(s1_scr), values_t(j + 1))

        def body(i, carry):
            pair(2 * i, False)
            return carry

        lax.fori_loop(0, n_chunks // 2 - 1, body, 0)
        pair(n_chunks - 2, True)
    if has_extra and n_chunks > 1:
        softmax_pv(lambda g: s0_scr[0:kx_ref.shape[1], g * tq:(g + 1) * tq], vxt_ref[0])
    elif has_extra:
        softmax_pv(from_value(scores(kx_ref[0])), vxt_ref[0])

    if n_comp == 2:
        lv = lam_ref[...]
        lam = (jnp.exp(jnp.sum(lv[0:1] * lv[1:2], axis=1, keepdims=True))
               - jnp.exp(jnp.sum(lv[2:3] * lv[3:4], axis=1, keepdims=True)) + lam_init)
    for h in range(N_HEADS):
        cols = slice(h * HEAD_W, (h + 1) * HEAD_W)
        if n_comp == 2:
            a0 = acc_scr[2 * h * VT_ROWS:(2 * h + 1) * VT_ROWS]
            a1 = acc_scr[(2 * h + 1) * VT_ROWS:(2 * h + 2) * VT_ROWS]
            o = (a0[:HEAD_W] / a0[HEAD_W:HEAD_W + 1]
                 - lam * (a1[:HEAD_W] / a1[HEAD_W:HEAD_W + 1]))
            o = (o * lax.rsqrt(jnp.mean(o * o, axis=0, keepdims=True) + EPS)
                 * gain_ref[...] * (1.0 - lam_init))
        else:
            a0 = acc_scr[h * VT_ROWS:(h + 1) * VT_ROWS]
            o = a0[:HEAD_W] / a0[HEAD_W:HEAD_W + 1]
        o_ref[0, :, cols] = o.T.astype(BF16)


def _flash(q, k, vt, extra, diff, tq, tk, lam_init=0.0):
    b, t, _ = q.shape
    tkeys = k.shape[1]
    n_comp = 2 if diff is not None else 1
    ng = N_HEADS * n_comp
    in_specs = [pl.BlockSpec((1, tq, QK_W), lambda bb, i: (bb, i, 0)),
                pl.BlockSpec((1, tkeys, QK_W), lambda bb, i: (bb, 0, 0)),
                pl.BlockSpec((1, VT_W, tkeys), lambda bb, i: (bb, 0, 0))]
    args = [q, k, vt]
    if extra is not None:
        lx = extra[0].shape[1]
        in_specs += [pl.BlockSpec((1, lx, QK_W), lambda bb, i: (bb, 0, 0)),
                     pl.BlockSpec((1, VT_W, lx), lambda bb, i: (bb, 0, 0))]
        args += list(extra)
    if diff is not None:
        in_specs += [pl.BlockSpec((4, DA_QK_DIM), lambda bb, i: (0, 0)),
                     pl.BlockSpec((HEAD_W, 1), lambda bb, i: (0, 0))]
        args += list(diff)
    return pl.pallas_call(
        functools.partial(_flash_kernel, n_comp=n_comp, tk=tk, n_chunks=tkeys // tk,
                          has_extra=extra is not None, lam_init=lam_init),
        grid=(b, t // tq), in_specs=in_specs,
        out_specs=pl.BlockSpec((1, tq, QK_W), lambda bb, i: (bb, i, 0)),
        out_shape=jax.ShapeDtypeStruct((b, t, QK_W), BF16),
        scratch_shapes=[pltpu.VMEM((ng * tq, QK_W), BF16), pltpu.VMEM((ng, tq), F32),
                        pltpu.VMEM((ng * VT_ROWS, tq), F32),
                        pltpu.VMEM((tk, ng * tq), F32), pltpu.VMEM((tk, ng * tq), F32)],
        compiler_params=_cparams(2),
        name="diff_attention" if diff is not None else "dense_attention",
    )(*args)


def _gla_dir(q_ref, k_ref, v_ref, g_ref, o_ref, state, cum_ref, attn_scr, o_scr, upd_scr, sin_scr, n_chunks, reverse):
    c = GLA_CHUNK
    tb = n_chunks * c
    b_all = _dot_f32_rhs(cum_ref[...], g_ref[0], 2)
    b3 = b_all.reshape(n_chunks, c, QK_W)
    btot3 = b3[:, 0:1] if reverse else b3[:, c - 1:c]
    bmid3 = b3[:, c // 2:c // 2 + 1]
    q3 = q_ref[0].reshape(n_chunks, c, QK_W)
    k3 = k_ref[0].reshape(n_chunks, c, QK_W)
    q_in = (q3 * jnp.exp(b3 - bmid3)).reshape(tb, QK_W).astype(BF16)
    k_in = (k3 * jnp.exp(bmid3 - b3)).reshape(tb, QK_W).astype(BF16)
    q_st = (q3 * jnp.exp(b3)).reshape(tb, QK_W).astype(BF16)
    k_st = (k3 * jnp.exp(btot3 - b3)).reshape(tb, QK_W)
    dec_all = jnp.exp(btot3)
    v = v_ref[0]

    nb_i = min(GLA_INTRA, tb)
    t = lax.broadcasted_iota(jnp.int32, (nb_i, nb_i), 0)
    s = lax.broadcasted_iota(jnp.int32, (nb_i, nb_i), 1)
    keep = ((t // c) == (s // c)) & ((s >= t) if reverse else (s <= t))
    zero = jnp.zeros((nb_i, QK_W), BF16)
    hm_i = [_head_mask((nb_i, QK_W), h * GLA_DK, (h + 1) * GLA_DK) for h in range(N_HEADS)]
    blocks = [(r0, h) for r0 in range(0, tb, nb_i) for h in range(N_HEADS)]
    for n, (r0, h) in enumerate(blocks):
        attn = lax.dot_general(q_in[r0:r0 + nb_i], jnp.where(hm_i[h], k_in[r0:r0 + nb_i], zero), _NT,
                               preferred_element_type=F32)
        attn_scr[n] = jnp.where(keep, attn, 0.0).astype(BF16)
    for n, (r0, h) in enumerate(blocks):
        vr = slice(h * GLA_DV, (h + 1) * GLA_DV)
        o_scr[r0:r0 + nb_i, vr] = jnp.dot(attn_scr[n], v[r0:r0 + nb_i, vr], preferred_element_type=F32)

    order = list(range(n_chunks - 1, -1, -1) if reverse else range(n_chunks))
    for ci in order:
        r0 = ci * c
        k_t = k_st[r0:r0 + c].T.astype(BF16)
        for h in range(N_HEADS):
            vr = slice(h * GLA_DV, (h + 1) * GLA_DV)
            upd_scr[ci, h * GLA_DK:(h + 1) * GLA_DK, :] = jnp.dot(
                k_t[h * GLA_DK:(h + 1) * GLA_DK], v[r0:r0 + c, vr], preferred_element_type=F32)
    s_cur = state[...]
    for ci in order:
        sin_scr[ci] = s_cur.astype(BF16)
        dec = jnp.broadcast_to(dec_all[ci], (GLA_DV, QK_W)).T
        s_cur = dec * s_cur + upd_scr[ci]
    state[...] = s_cur
    qzero = jnp.zeros((c, QK_W), BF16)
    hmasks = [_head_mask((c, QK_W), h * GLA_DK, (h + 1) * GLA_DK) for h in range(N_HEADS)]
    for ci in order:
        r0 = ci * c
        qc = q_st[r0:r0 + c]
        q_stack = jnp.concatenate([jnp.where(hm, qc, qzero) for hm in hmasks], axis=0)
        o_st = jnp.dot(q_stack, sin_scr[ci], preferred_element_type=F32)
        for h in range(N_HEADS):
            vr = slice(h * GLA_DV, (h + 1) * GLA_DV)
            o_ref[0, r0:r0 + c, vr] = (o_scr[r0:r0 + c, vr] + o_st[h * c:(h + 1) * c]).astype(o_ref.dtype)


def _gla_kernel(qf_ref, kf_ref, vf_ref, gf_ref, qb_ref, kb_ref, vb_ref, gb_ref, s0_ref,
                cumf_ref, cumb_ref, of_ref, ob_ref, sfin_ref, sf, sb, attn_f, attn_b, o_f, o_b,
                upd_f, upd_b, sin_f, sin_b, *, n_chunks):
    i = pl.program_id(1)

    @pl.when(i == 0)
    def _():
        sf[...] = s0_ref[0, 0]
        sb[...] = s0_ref[0, 1]

    _gla_dir(qf_ref, kf_ref, vf_ref, gf_ref, of_ref, sf, cumf_ref, attn_f, o_f, upd_f, sin_f, n_chunks, False)
    _gla_dir(qb_ref, kb_ref, vb_ref, gb_ref, ob_ref, sb, cumb_ref, attn_b, o_b, upd_b, sin_b, n_chunks, True)

    @pl.when(i == pl.num_programs(1) - 1)
    def _():
        sfin_ref[0, 0] = sf[...]
        sfin_ref[0, 1] = sb[...]


def _gla(p, s0, tb):
    b, t, _ = p["qc"].shape
    nb = t // tb
    n_chunks = tb // GLA_CHUNK
    nb_i = min(GLA_INTRA, tb)
    n_blk = (tb // nb_i) * N_HEADS
    idx = np.arange(tb)
    same = (idx[:, None] // GLA_CHUNK) == (idx[None, :] // GLA_CHUNK)
    cumf = jnp.asarray(same & (idx[None, :] <= idx[:, None]), BF16)
    cumb = jnp.asarray(same & (idx[None, :] >= idx[:, None]), BF16)
    fwd = lambda w: pl.BlockSpec((1, tb, w), lambda bb, i: (bb, i, 0))
    bwd = lambda w: pl.BlockSpec((1, tb, w), lambda bb, i: (bb, nb - 1 - i, 0))
    st = pl.BlockSpec((1, 2, QK_W, GLA_DV), lambda bb, i: (bb, 0, 0, 0))
    cm = pl.BlockSpec((tb, tb), lambda bb, i: (0, 0))
    sd = jax.ShapeDtypeStruct
    return pl.pallas_call(
        functools.partial(_gla_kernel, n_chunks=n_chunks),
        grid=(b, nb),
        in_specs=[fwd(QK_W), fwd(QK_W), fwd(GLA_V_W), fwd(QK_W),
                  bwd(QK_W), bwd(QK_W), bwd(GLA_V_W), bwd(QK_W), st, cm, cm],
        out_specs=[fwd(GLA_V_W), bwd(GLA_V_W), st],
        out_shape=[sd((b, t, GLA_V_W), BF16), sd((b, t, GLA_V_W), BF16), sd((b, 2, QK_W, GLA_DV), F32)],
        scratch_shapes=[pltpu.VMEM((QK_W, GLA_DV), F32), pltpu.VMEM((QK_W, GLA_DV), F32)]
        + [pltpu.VMEM((n_blk, nb_i, nb_i), BF16)] * 2 + [pltpu.VMEM((tb, GLA_V_W), F32)] * 2
        + [pltpu.VMEM((n_chunks, QK_W, GLA_DV), F32)] * 2 + [pltpu.VMEM((n_chunks, QK_W, GLA_DV), BF16)] * 2,
        compiler_params=_cparams(2), name="gla_scan",
    )(p["qc"], p["kc"], p["vc"], p["gf"], p["qc"], p["kc"], p["vc"], p["gb"], s0, cumf, cumb)


def _outproj_kernel(h_ref, oa_ref, ob_ref, ocf_ref, ocb_ref, gc_ref, on_ref, w_ref, g1_ref, o_ref):
    oc = ocf_ref[0].astype(F32) + ocb_ref[0].astype(F32)
    gate = gc_ref[0]
    parts = []
    for h in range(N_HEADS):
        vr = slice(h * GLA_DV, (h + 1) * GLA_DV)
        x = oc[:, vr]
        x = x * lax.rsqrt(jnp.mean(x * x, axis=-1, keepdims=True) + EPS) * on_ref[...]
        parts.append((x * _silu(gate[:, vr])).astype(BF16))
    y = (jnp.dot(oa_ref[0], w_ref[0:QK_W], preferred_element_type=F32)
         + jnp.dot(ob_ref[0], w_ref[QK_W:2 * QK_W], preferred_element_type=F32)
         + jnp.dot(jnp.concatenate(parts, axis=1), w_ref[2 * QK_W:], preferred_element_type=F32))
    o_ref[0] = h_ref[0] + g1_ref[0] * y


def _outproj(h, oa, ob, ocf, ocb, gc, onorm, w_out, g1, tm):
    b, t, d = h.shape
    tok = lambda w: pl.BlockSpec((1, tm, w), lambda bb, i: (bb, i, 0))
    return pl.pallas_call(
        _outproj_kernel, grid=(b, t // tm),
        in_specs=[tok(d), tok(QK_W), tok(QK_W), tok(GLA_V_W), tok(GLA_V_W), tok(GLA_V_W),
                  pl.BlockSpec((1, GLA_DV), lambda bb, i: (0, 0)),
                  pl.BlockSpec(w_out.shape, lambda bb, i: (0, 0)),
                  pl.BlockSpec((1, 1, d), lambda bb, i: (bb, 0, 0))],
        out_specs=tok(d), out_shape=jax.ShapeDtypeStruct((b, t, d), F32),
        compiler_params=_cparams(2), name="out_proj",
    )(h, oa, ob, ocf, ocb, gc, onorm, w_out, g1)


FFN_HALO = 16
FFN_CHUNKS = ((0, 1280), (1280, 1536))


def _ffn_kernel(h_ref, hp_ref, hn_ref, sc_ref, sh_ref, n2_ref, wg_ref, wu_ref, wd_ref, cw_ref, cb_ref,
                g2_ref, o_ref, xn_scr, a_scr, *, chunks):
    i = pl.program_id(1)
    nt = pl.num_programs(1)
    tm = h_ref.shape[1]
    n2 = n2_ref[...]
    sc = sc_ref[0]
    sh = sh_ref[0]

    def normed(x):
        y = x * lax.rsqrt(jnp.mean(x * x, axis=-1, keepdims=True) + EPS) * n2
        return y * (1.0 + sc) + sh

    x = h_ref[0]
    prev_ok = (i > 0).astype(F32)
    next_ok = (i < nt - 1).astype(F32)
    xn_scr[0:FFN_HALO] = (normed(hp_ref[0]) * prev_ok).astype(BF16)
    xn_scr[FFN_HALO:FFN_HALO + tm] = normed(x).astype(BF16)
    xn_scr[FFN_HALO + tm:] = (normed(hn_ref[0]) * next_ok).astype(BF16)

    acc = None
    for f0, fc in chunks:
        a_scr[:, :fc] = jnp.dot(xn_scr[...], wg_ref[:, f0:f0 + fc], preferred_element_type=F32)
        u = jnp.dot(xn_scr[FFN_HALO:FFN_HALO + tm], wu_ref[:, f0:f0 + fc], preferred_element_type=F32)
        cw = cw_ref[:, f0:f0 + fc]
        a = (cb_ref[:, f0:f0 + fc]
             + a_scr[FFN_HALO - 1:FFN_HALO - 1 + tm, :fc] * cw[0:1]
             + a_scr[FFN_HALO:FFN_HALO + tm, :fc] * cw[1:2]
             + a_scr[FFN_HALO + 1:FFN_HALO + 1 + tm, :fc] * cw[2:3])
        g = (_silu(a) * u).astype(BF16)
        t = jnp.dot(g, wd_ref[f0:f0 + fc], preferred_element_type=F32)
        acc = t if acc is None else acc + t
    o_ref[0] = x + g2_ref[0] * acc


def _ffn(h, sc, sh, n2, wg, wu, wd, cw, cb, g2, tm):
    b, t, d = h.shape
    nt = t // tm
    hb = tm // FFN_HALO
    tok = pl.BlockSpec((1, tm, d), lambda bb, i: (bb, i, 0))
    prev = pl.BlockSpec((1, FFN_HALO, d), lambda bb, i: (bb, jnp.maximum(i * hb - 1, 0), 0))
    nxt = pl.BlockSpec((1, FFN_HALO, d), lambda bb, i: (bb, jnp.minimum((i + 1) * hb, nt * hb - 1), 0))
    mod = pl.BlockSpec((1, 1, d), lambda bb, i: (bb, 0, 0))
    const = lambda shape: pl.BlockSpec(shape, lambda bb, i: (0,) * len(shape),
                                       pipeline_mode=pl.Buffered(1))
    return pl.pallas_call(
        functools.partial(_ffn_kernel, chunks=FFN_CHUNKS), grid=(b, nt),
        in_specs=[tok, prev, nxt, mod, mod, const((1, d)), const(wg.shape), const(wu.shape),
                  const(wd.shape), const(cw.shape), const(cb.shape), mod],
        out_specs=tok, out_shape=jax.ShapeDtypeStruct((b, t, d), F32),
        scratch_shapes=[pltpu.VMEM((tm + 2 * FFN_HALO, d), BF16),
                        pltpu.VMEM((tm + 2 * FFN_HALO, max(fc for _, fc in FFN_CHUNKS)), F32)],
        compiler_params=_cparams(2), name="conv_ffn",
    )(h, h, h, sc, sh, n2, wg, wu, wd, cw, cb, g2)


def _rope_tables(s):
    t = np.arange(s)
    row, col = t // GRID_W, t % GRID_W
    lane = np.arange(QK_W)
    jj = lane % DA_QK_DIM
    nf = DA_QK_DIM // 4
    inv = jnp.asarray(ROPE_THETA, F32) ** (-jnp.arange(nf, dtype=F32) / nf)
    pos = jnp.where((jj // (2 * nf) == 0)[None, :], row[:, None], col[:, None]).astype(F32)
    ang = pos * inv[jj % nf][None, :]
    first = ((jj % (2 * nf)) < nf)[None, :]
    cos, sin = jnp.cos(ang), jnp.sin(ang)
    return cos, jnp.where(first, -sin, 0.0), jnp.where(first, 0.0, sin)


def _block_ones(gsz):
    g = np.arange(QK_W) // gsz
    return jnp.asarray(g[:, None] == g[None, :], BF16)


def _layer_weights(l, w_in, qn_a, kn_a, qn_b, kn_b, w_a2_f, b_a_f, w_a2_b, b_a_b):
    w = w_in[l]
    d = w.shape[0]
    wn = jnp.concatenate([w[:, 0:512], w[:, 768:1280], w[:, 1536:3104],
                          jnp.zeros((d, GATE_PAD - 2 * GLA_GATE_RANK), F32)], axis=1).astype(BF16)
    wt = jnp.concatenate([w[:, 512:768], w[:, 1280:1536]], axis=1).T.astype(BF16)
    gains = jnp.stack([jnp.tile(qn_a[l], N_HEADS), jnp.tile(kn_a[l], N_HEADS),
                       jnp.tile(qn_b[l], 2 * N_HEADS), jnp.tile(kn_b[l], 2 * N_HEADS)])
    w2 = jnp.zeros((GATE_PAD, 2 * QK_W), F32)
    w2 = w2.at[0:GLA_GATE_RANK, 0:QK_W].set(w_a2_f[l])
    w2 = w2.at[GLA_GATE_RANK:2 * GLA_GATE_RANK, QK_W:].set(w_a2_b[l])
    w2_hi = w2.astype(BF16)
    w2_lo = (w2 - w2_hi.astype(F32)).astype(BF16)
    b2 = jnp.concatenate([b_a_f[l], b_a_b[l]])[None, :]
    return {"wn": wn, "wt": wt, "g64": _block_ones(HEAD_W), "g32": _block_ones(DA_QK_DIM),
            "gains": gains, "w2": jnp.stack([w2_hi, w2_lo]), "b2": b2}


def kernel(x, c, ctx, c_ctx, norm1, norm2, w_ada, b_ada, w_in, qn_a, kn_a, rpb_a, qn_b, kn_b,
           lam_q1, lam_k1, lam_q2, lam_k2, subln_b, w_a2_f, b_a_f, w_a2_b, b_a_b, onorm_c, w_out,
           w_g, w_u, conv_w, conv_b, w_d):
    bsz, s, d = x.shape
    lc = ctx.shape[1]
    depth = w_in.shape[0]
    rows = s // GRID_W
    tm = min(512, s)
    tb = min(512, s)

    cvec = jnp.zeros((16, d), F32).at[:bsz].set(c).at[bsz].set(c_ctx)
    ada = _ada(cvec, w_ada, b_ada)
    rope = _rope_tables(s)

    h, hc = x, ctx
    for l in range(depth):
        with_ctx_out = l < depth - 1
        lam_init = 0.8 - 0.6 * math.exp(-0.3 * l)
        m = ada[l, :bsz].reshape(bsz, 1, 6, d)
        mc = jnp.broadcast_to(ada[l, bsz].reshape(1, 1, 6, d), (bsz, 1, 6, d))
        sh1, sc1, g1, sh2, sc2, g2 = (m[:, :, j] for j in range(6))
        csh1, csc1, cg1, csh2, csc2, cg2 = (mc[:, :, j] for j in range(6))
        wts = _layer_weights(l, w_in, qn_a, kn_a, qn_b, kn_b, w_a2_f, b_a_f, w_a2_b, b_a_b)
        n1 = norm1[l][None, :]
        n2 = norm2[l][None, :]

        pl_ = _inproj(h, sc1, sh1, n1, wts, rope, tm)
        pc_ = _inproj(hc, csc1, csh1, n1, wts, None, lc)

        tab = _na_table(rpb_a[l])
        o_a = _na_attention(pl_["qa"], pl_["ka"], pl_["vat"], pc_["ka"], pc_["vat"], tab)
        lamv = jnp.stack([lam_q1[l], lam_k1[l], lam_q2[l], lam_k2[l]])
        diff = (lamv, subln_b[l][:, None])
        o_b = _flash(pl_["qb"], pl_["kb"], pl_["vbt"], (pc_["kb"], pc_["vbt"]), diff, 256, 512, lam_init)

        s0 = jnp.zeros((bsz, 2, QK_W, GLA_DV), F32)
        ocf_c, ocb_c, s_ctx = _gla(pc_, s0, lc)
        ocf, ocb, _ = _gla(pl_, s_ctx, tb)

        w_o = w_out[l].astype(BF16)
        on = onorm_c[l][None, :]
        wg, wu, wd = w_g[l].astype(BF16), w_u[l].astype(BF16), w_d[l].astype(BF16)
        cw, cb = conv_w[l], conv_b[l][None, :]
        h = _outproj(h, o_a, o_b, ocf, ocb, pl_["gc"], on, w_o, g1, tm)
        h = _ffn(h, sc2, sh2, n2, wg, wu, wd, cw, cb, g2, tm)
        if with_ctx_out:
            o_a_c = _flash(pc_["qa"], pc_["ka"], pc_["vat"], None, None, lc, lc)
            o_b_c = _flash(pc_["qb"], pc_["kb"], pc_["vbt"], None, diff, lc, lc, lam_init)
            hc = _outproj(hc, o_a_c, o_b_c, ocf_c, ocb_c, pc_["gc"], on, w_o, cg1, lc)
            hc = _ffn(hc, csc2, csh2, n2, wg, wu, wd, cw, cb, cg2, lc)
    return h
```

```python
import functools
import math

import jax
import jax.numpy as jnp
import numpy as np
from jax import lax
from jax.experimental import pallas as pl
from jax.experimental.pallas import tpu as pltpu

F32 = jnp.float32
BF16 = jnp.bfloat16

D_MODEL = 1024
GRID_W = 64
HEAD_W = 64
N_HEADS = 4
NA_WIN_ROWS = 8
NA_WIN_COLS = 16
NA_Q_ROWS = 4
NA_K_ROWS = 12
DA_QK_DIM = 32
GLA_DK = 64
GLA_DV = 128
GLA_CHUNK = 64
GLA_INTRA = 128
GLA_GATE_RANK = 16
GLA_GATE_NORM = 16.0
GATE_PAD = 128
FFN_DIM = 2816
ROPE_THETA = 10000.0
EPS = 1e-6
NEG_INF = -1e30
LOG2E = math.log2(math.e)
QK_W = N_HEADS * HEAD_W
VT_ROWS = HEAD_W + 16
VT_W = N_HEADS * VT_ROWS
GLA_V_W = N_HEADS * GLA_DV
VMEM_LIMIT = 56 * 1024 * 1024

_NT = (((1,), (1,)), ((), ()))


def _cparams(n_axes):
    return pltpu.CompilerParams(dimension_semantics=("arbitrary",) * n_axes,
                                vmem_limit_bytes=VMEM_LIMIT)


def _split_bf16(x, parts):
    out = []
    r = x
    for _ in range(parts):
        p = r.astype(BF16)
        out.append(p)
        r = r - p.astype(F32)
    return out


def _dot_f32_lhs(x, m_bf16, parts):
    acc = None
    for p in _split_bf16(x, parts):
        t = jnp.dot(p, m_bf16, preferred_element_type=F32)
        acc = t if acc is None else acc + t
    return acc


def _dot_f32_rhs(m_bf16, x, parts):
    acc = None
    for p in _split_bf16(x, parts):
        t = jnp.dot(m_bf16, p, preferred_element_type=F32)
        acc = t if acc is None else acc + t
    return acc


def _silu(x):
    return x * (1.0 / (1.0 + jnp.exp(-x)))


def _log_sigmoid(x):
    return jnp.minimum(x, 0.0) - jnp.log(1.0 + jnp.exp(-jnp.abs(x)))


def _ada_kernel(c_ref, w_ref, b_ref, o_ref):
    s = _silu(c_ref[...])
    w = w_ref[0]
    acc = None
    for sp in _split_bf16(s, 3):
        for wp in _split_bf16(w, 2):
            t = jnp.dot(sp, wp, preferred_element_type=F32)
            acc = t if acc is None else acc + t
    o_ref[0] = acc + b_ref[0]


def _ada(cvec, w_ada, b_ada):
    n_l, d, n6 = w_ada.shape
    tn = 1536
    return pl.pallas_call(
        _ada_kernel,
        grid=(n_l, n6 // tn),
        in_specs=[pl.BlockSpec((16, d), lambda l, j: (0, 0)),
                  pl.BlockSpec((1, d, tn), lambda l, j: (l, 0, j)),
                  pl.BlockSpec((1, 1, tn), lambda l, j: (l, 0, j))],
        out_specs=pl.BlockSpec((1, 16, tn), lambda l, j: (l, 0, j)),
        out_shape=jax.ShapeDtypeStruct((n_l, 16, n6), F32),
        compiler_params=_cparams(2),
        name="ada_proj",
    )(cvec, w_ada, b_ada.reshape(n_l, 1, n6))


def _group_rms(x, gmat, gain, gsz):
    ssq = _dot_f32_lhs(x * x, gmat, 1)
    return x * lax.rsqrt(ssq * (1.0 / gsz) + EPS) * gain


def _inproj_kernel(h_ref, sc_ref, sh_ref, n1_ref, wn_ref, wt_ref, g64_ref, g32_ref, gains_ref,
                   w2_ref, b2_ref, *rest, rope):
    if rope:
        cos_ref, s1_ref, s2_ref = rest[:3]
        rest = rest[3:]
    (qa_ref, ka_ref, qb_ref, kb_ref, qc_ref, kc_ref, vc_ref, gc_ref, gf_ref, gb_ref,
     vat_ref, vbt_ref, y_scr) = rest

    x = h_ref[0]
    y = x * lax.rsqrt(jnp.mean(x * x, axis=-1, keepdims=True) + EPS) * n1_ref[...]
    xn = (y * (1.0 + sc_ref[0]) + sh_ref[0]).astype(BF16)

    y_scr[...] = jnp.dot(xn, wn_ref[...], preferred_element_type=F32)
    vt = lax.dot_general(wt_ref[...], xn, _NT, preferred_element_type=F32)

    def proj(lo, hi):
        return y_scr[:, lo:hi]

    g64 = g64_ref[...]
    g32 = g32_ref[...]
    gains = gains_ref[...]

    def rot(v):
        if not rope:
            return v
        return (v * cos_ref[...] + pltpu.roll(v, QK_W - 8, 1) * s1_ref[...]
                + pltpu.roll(v, 8, 1) * s2_ref[...])

    qa_ref[0] = (_group_rms(proj(0, 256), g64, gains[0:1], HEAD_W) * (HEAD_W ** -0.5 * LOG2E)).astype(BF16)
    ka_ref[0] = _group_rms(proj(256, 512), g64, gains[1:2], HEAD_W).astype(BF16)
    qb_ref[0] = (rot(_group_rms(proj(512, 768), g32, gains[2:3], DA_QK_DIM))
                 * (DA_QK_DIM ** -0.5 * LOG2E)).astype(BF16)
    kb_ref[0] = rot(_group_rms(proj(768, 1024), g32, gains[3:4], DA_QK_DIM)).astype(BF16)
    qc_ref[0] = proj(1024, 1280) * (GLA_DK ** -0.5)
    kc_ref[0] = proj(1280, 1536)
    vc_ref[0] = proj(1536, 2048).astype(BF16)
    gc_ref[0] = proj(2048, 2560)
    a_lr = proj(2560, 2560 + GATE_PAD)
    a_hi, a_lo = _split_bf16(a_lr, 2)
    pre = (jnp.dot(a_hi, w2_ref[0], preferred_element_type=F32)
           + jnp.dot(a_lo, w2_ref[0], preferred_element_type=F32)
           + jnp.dot(a_hi, w2_ref[1], preferred_element_type=F32))
    gate = _log_sigmoid(pre + b2_ref[...]) * (1.0 / GLA_GATE_NORM)
    gf_ref[0] = gate[:, :QK_W]
    gb_ref[0] = gate[:, QK_W:]
    ones = jnp.ones((VT_ROWS - HEAD_W, vt.shape[1]), BF16)
    for grp, ref in enumerate((vat_ref, vbt_ref)):
        for h in range(N_HEADS):
            src = (grp * N_HEADS + h) * HEAD_W
            ref[0, h * VT_ROWS:h * VT_ROWS + HEAD_W, :] = vt[src:src + HEAD_W].astype(BF16)
            ref[0, h * VT_ROWS + HEAD_W:(h + 1) * VT_ROWS, :] = ones


def _inproj(h, sc, sh, n1, wts, rope_tabs, tm):
    b, t, d = h.shape
    nt = t // tm
    rope = rope_tabs is not None
    const = lambda shape: pl.BlockSpec(shape, lambda i, bb: (0,) * len(shape))
    tok = lambda w: pl.BlockSpec((1, tm, w), lambda i, bb: (bb, i, 0))
    mod = pl.BlockSpec((1, 1, d), lambda i, bb: (bb, 0, 0))
    in_specs = [tok(d), mod, mod, const((1, d)), const(wts["wn"].shape), const(wts["wt"].shape),
                const((QK_W, QK_W)), const((QK_W, QK_W)), const((4, QK_W)),
                const((2, GATE_PAD, 2 * QK_W)), const((1, 2 * QK_W))]
    args = [h, sc, sh, n1, wts["wn"], wts["wt"], wts["g64"], wts["g32"], wts["gains"],
            wts["w2"], wts["b2"]]
    if rope:
        in_specs += [pl.BlockSpec((tm, QK_W), lambda i, bb: (i, 0))] * 3
        args += list(rope_tabs)
    tspec = pl.BlockSpec((1, VT_W, tm), lambda i, bb: (bb, 0, i))
    out_specs = [tok(QK_W)] * 6 + [tok(GLA_V_W), tok(GLA_V_W), tok(QK_W), tok(QK_W), tspec, tspec]
    sd = jax.ShapeDtypeStruct
    out_shape = [sd((b, t, QK_W), BF16)] * 4 + [sd((b, t, QK_W), F32)] * 2 + [
        sd((b, t, GLA_V_W), BF16), sd((b, t, GLA_V_W), F32), sd((b, t, QK_W), F32),
        sd((b, t, QK_W), F32), sd((b, VT_W, t), BF16), sd((b, VT_W, t), BF16)]
    outs = pl.pallas_call(
        functools.partial(_inproj_kernel, rope=rope),
        grid=(nt, b), in_specs=in_specs, out_specs=out_specs, out_shape=out_shape,
        scratch_shapes=[pltpu.VMEM((tm, wts["wn"].shape[1]), F32)],
        compiler_params=_cparams(2), name="in_proj_rope" if rope else "in_proj",
    )(*args)
    names = ("qa", "ka", "qb", "kb", "qc", "kc", "vc", "gc", "gf", "gb", "vat", "vbt")
    return dict(zip(names, outs))


def _head_mask(shape, lo, hi):
    lane = lax.broadcasted_iota(jnp.int32, shape, 1)
    return (lane >= lo) & (lane < hi)


def _na_kernel(q_ref, k_ref, vt_ref, kc_ref, vct_ref, tab_ref, o_ref, qm_scr, s_scr, *, n_rows):
    i = pl.program_id(1)
    nq = NA_Q_ROWS * GRID_W
    nk = NA_K_ROWS * GRID_W
    base_row = jnp.clip(NA_Q_ROWS * i - NA_WIN_ROWS // 2, 0, n_rows - NA_K_ROWS)
    base = pl.multiple_of(base_row * GRID_W, 256)
    btype = jnp.where(i == 0, 0, jnp.where(i == pl.num_programs(1) - 1, 2, 1))
    kw = k_ref[0, pl.ds(base, nk), :]
    vw = vt_ref[0, :, pl.ds(base, nk)]
    kc = kc_ref[0]
    vct = vct_ref[0]
    q = q_ref[0]
    zero = jnp.zeros_like(q)
    for h in range(N_HEADS):
        qm_scr[h * nq:(h + 1) * nq, :] = jnp.where(_head_mask(q.shape, h * HEAD_W, (h + 1) * HEAD_W), q, zero)
    s_scr[...] = lax.dot_general(jnp.concatenate([kw, kc], axis=0), qm_scr[...], _NT,
                                 preferred_element_type=F32)
    vt_all = jnp.concatenate([vw, vct], axis=1)
    for h in range(N_HEADS):
        cols = slice(h * nq, (h + 1) * nq)
        sw = s_scr[0:nk, cols] + tab_ref[btype, h]
        sc = s_scr[nk:, cols]
        m = jnp.maximum(jnp.max(sw, axis=0, keepdims=True), jnp.max(sc, axis=0, keepdims=True))
        p = jnp.concatenate([jnp.exp2(sw - m), jnp.exp2(sc - m)], axis=0).astype(BF16)
        ot = jnp.dot(vt_all[h * VT_ROWS:(h + 1) * VT_ROWS], p, preferred_element_type=F32)
        o_ref[0, :, h * HEAD_W:(h + 1) * HEAD_W] = (ot[:HEAD_W] / ot[HEAD_W:HEAD_W + 1]).T.astype(BF16)


RPB_ROWS = 2 * NA_WIN_ROWS - 1
RPB_COLS = 2 * NA_WIN_COLS - 1


def _na_table_kernel(rpb_ref, o_ref, *, n_rows):
    h = pl.program_id(0)
    c = lax.broadcasted_iota(jnp.int32, (GRID_W, GRID_W), 0)
    w = lax.broadcasted_iota(jnp.int32, (GRID_W, GRID_W), 1)
    co = jnp.clip(c - w, -(NA_WIN_COLS - 1), NA_WIN_COLS - 1) + NA_WIN_COLS - 1
    c0 = jnp.clip(w - NA_WIN_COLS // 2, 0, GRID_W - NA_WIN_COLS)
    col_ok = (c >= c0) & (c < c0 + NA_WIN_COLS)
    tiles = []
    for ro in range(RPB_ROWS):
        t = jnp.zeros((GRID_W, GRID_W), F32)
        for kk in range(RPB_COLS):
            t = jnp.where(co == kk, rpb_ref[(h * RPB_ROWS + ro) * RPB_COLS + kk], t)
        tiles.append(jnp.where(col_ok, t * LOG2E, NEG_INF))
    outside = jnp.full((GRID_W, GRID_W), NEG_INF, F32)
    n_blocks = n_rows // NA_Q_ROWS
    for bt, blk in enumerate((0, 1, n_blocks - 1)):
        base_row = min(max(NA_Q_ROWS * blk - NA_WIN_ROWS // 2, 0), n_rows - NA_K_ROWS)
        for a in range(NA_K_ROWS):
            for b in range(NA_Q_ROWS):
                key_row, q_row = base_row + a, NA_Q_ROWS * blk + b
                r0 = min(max(q_row - NA_WIN_ROWS // 2, 0), n_rows - NA_WIN_ROWS)
                inside = r0 <= key_row < r0 + NA_WIN_ROWS
                o_ref[bt, 0, a * GRID_W:(a + 1) * GRID_W, b * GRID_W:(b + 1) * GRID_W] = (
                    tiles[key_row - q_row + NA_WIN_ROWS - 1] if inside else outside)


def _na_table(rpb, n_rows):
    n_h = rpb.shape[0]
    nk, nq = NA_K_ROWS * GRID_W, NA_Q_ROWS * GRID_W
    flat = jnp.zeros((2048,), F32).at[:n_h * RPB_ROWS * RPB_COLS].set(rpb.reshape(-1))
    return pl.pallas_call(
        functools.partial(_na_table_kernel, n_rows=n_rows), grid=(n_h,),
        in_specs=[pl.BlockSpec(memory_space=pltpu.SMEM)],
        out_specs=pl.BlockSpec((3, 1, nk, nq), lambda h: (0, h, 0, 0)),
        out_shape=jax.ShapeDtypeStruct((3, n_h, nk, nq), F32),
        compiler_params=_cparams(1), name="na_bias_table",
    )(flat)


def _na_attention(q, k, vt, kc, vct, tab):
    b, s, _ = q.shape
    rows = s // GRID_W
    nb = rows // NA_Q_ROWS
    nq = NA_Q_ROWS * GRID_W
    lc = kc.shape[1]
    return pl.pallas_call(
        functools.partial(_na_kernel, n_rows=rows),
        grid=(b, nb),
        in_specs=[pl.BlockSpec((1, nq, QK_W), lambda bb, i: (bb, i, 0)),
                  pl.BlockSpec((1, s, QK_W), lambda bb, i: (bb, 0, 0)),
                  pl.BlockSpec((1, VT_W, s), lambda bb, i: (bb, 0, 0)),
                  pl.BlockSpec((1, lc, QK_W), lambda bb, i: (bb, 0, 0)),
                  pl.BlockSpec((1, VT_W, lc), lambda bb, i: (bb, 0, 0)),
                  pl.BlockSpec(tab.shape, lambda bb, i: (0, 0, 0, 0), pipeline_mode=pl.Buffered(1))],
        out_specs=pl.BlockSpec((1, nq, QK_W), lambda bb, i: (bb, i, 0)),
        out_shape=jax.ShapeDtypeStruct((b, s, QK_W), BF16),
        scratch_shapes=[pltpu.VMEM((N_HEADS * nq, QK_W), BF16),
                        pltpu.VMEM((NA_K_ROWS * GRID_W + lc, N_HEADS * nq), F32)],
        compiler_params=_cparams(2), name="na_attention",
    )(q, k, vt, kc, vct, tab)


def _flash_kernel(*refs, n_comp, tk, n_chunks, has_extra, lam_init):
    q_ref, k_ref, vt_ref = refs[:3]
    refs = refs[3:]
    if has_extra:
        kx_ref, vxt_ref = refs[:2]
        refs = refs[2:]
    if n_comp == 2:
        lam_ref, gain_ref = refs[:2]
        refs = refs[2:]
    o_ref, qm_scr, m_scr, acc_scr, s0_scr, s1_scr, c0_scr, c1_scr = refs

    q = q_ref[0]
    tq = q.shape[0]
    dsub = HEAD_W // n_comp
    ng = N_HEADS * n_comp
    zero = jnp.zeros_like(q)
    for g in range(ng):
        qm_scr[g * tq:(g + 1) * tq, :] = jnp.where(_head_mask(q.shape, g * dsub, (g + 1) * dsub), q, zero)
    m_scr[...] = jnp.full(m_scr.shape, NEG_INF, F32)
    acc_scr[...] = jnp.zeros(acc_scr.shape, F32)

    def issue_scores(dst, cmax, kc):
        nkeys = kc.shape[0]

        def issue(g):
            sg = lax.dot_general(kc, qm_scr[g * tq:(g + 1) * tq, :], _NT, preferred_element_type=F32)
            dst[0:nkeys, g * tq:(g + 1) * tq] = sg
            cmax[g:g + 1] = jnp.max(sg, axis=0, keepdims=True)
        return issue

    def softmax_pv(src, cmax, vtc, before_group=None):
        nkeys = vtc.shape[1]
        for g in range(ng):
            if before_group is not None:
                before_group(g)
            h = g // n_comp
            m_old = m_scr[g:g + 1]
            m_new = jnp.maximum(m_old, cmax[g:g + 1])
            alpha = jnp.exp2(m_old - m_new)
            p = jnp.exp2(src[0:nkeys, g * tq:(g + 1) * tq] - m_new)
            rows = slice(g * VT_ROWS, (g + 1) * VT_ROWS)
            acc_scr[rows] = alpha * acc_scr[rows] + jnp.dot(
                vtc[h * VT_ROWS:(h + 1) * VT_ROWS], p.astype(BF16), preferred_element_type=F32)
            m_scr[g:g + 1] = m_new

    def keys(j):
        return k_ref[0, pl.ds(pl.multiple_of(j * tk, tk), tk), :]

    def values_t(j):
        return vt_ref[0, :, pl.ds(pl.multiple_of(j * tk, tk), tk)]

    first = issue_scores(s0_scr, c0_scr, keys(0))
    for g in range(ng):
        first(g)
    if n_chunks > 1:
        def pair(j, last):
            softmax_pv(s0_scr, c0_scr, values_t(j), issue_scores(s1_scr, c1_scr, keys(j + 1)))
            if not last:
                nxt = issue_scores(s0_scr, c0_scr, keys(j + 2))
            elif has_extra:
                nxt = issue_scores(s0_scr, c0_scr, kx_ref[0])
            else:
                nxt = None
            softmax_pv(s1_scr, c1_scr, values_t(j + 1), nxt)

        def body(i, carry):
            pair(2 * i, False)
            return carry

        lax.fori_loop(0, n_chunks // 2 - 1, body, 0)
        pair(n_chunks - 2, True)
        if has_extra:
            softmax_pv(s0_scr, c0_scr, vxt_ref[0])
    else:
        nxt = issue_scores(s1_scr, c1_scr, kx_ref[0]) if has_extra else None
        softmax_pv(s0_scr, c0_scr, values_t(0), nxt)
        if has_extra:
            softmax_pv(s1_scr, c1_scr, vxt_ref[0])

    if n_comp == 2:
        lv = lam_ref[...]
        lam = (jnp.exp(jnp.sum(lv[0:1] * lv[1:2], axis=1, keepdims=True))
               - jnp.exp(jnp.sum(lv[2:3] * lv[3:4], axis=1, keepdims=True)) + lam_init)
    for h in range(N_HEADS):
        cols = slice(h * HEAD_W, (h + 1) * HEAD_W)
        if n_comp == 2:
            a0 = acc_scr[2 * h * VT_ROWS:(2 * h + 1) * VT_ROWS]
            a1 = acc_scr[(2 * h + 1) * VT_ROWS:(2 * h + 2) * VT_ROWS]
            o = (a0[:HEAD_W] / a0[HEAD_W:HEAD_W + 1]
                 - lam * (a1[:HEAD_W] / a1[HEAD_W:HEAD_W + 1]))
            o = (o * lax.rsqrt(jnp.mean(o * o, axis=0, keepdims=True) + EPS)
                 * gain_ref[...] * (1.0 - lam_init))
        else:
            a0 = acc_scr[h * VT_ROWS:(h + 1) * VT_ROWS]
            o = a0[:HEAD_W] / a0[HEAD_W:HEAD_W + 1]
        o_ref[0, :, cols] = o.T.astype(BF16)


def _flash(q, k, vt, extra, diff, tq, tk, lam_init=0.0):
    b, t, _ = q.shape
    tkeys = k.shape[1]
    n_comp = 2 if diff is not None else 1
    ng = N_HEADS * n_comp
    in_specs = [pl.BlockSpec((1, tq, QK_W), lambda bb, i: (bb, i, 0)),
                pl.BlockSpec((1, tkeys, QK_W), lambda bb, i: (bb, 0, 0)),
                pl.BlockSpec((1, VT_W, tkeys), lambda bb, i: (bb, 0, 0))]
    args = [q, k, vt]
    if extra is not None:
        lx = extra[0].shape[1]
        in_specs += [pl.BlockSpec((1, lx, QK_W), lambda bb, i: (bb, 0, 0)),
                     pl.BlockSpec((1, VT_W, lx), lambda bb, i: (bb, 0, 0))]
        args += list(extra)
    if diff is not None:
        in_specs += [pl.BlockSpec((4, DA_QK_DIM), lambda bb, i: (0, 0)),
                     pl.BlockSpec((HEAD_W, 1), lambda bb, i: (0, 0))]
        args += list(diff)
    return pl.pallas_call(
        functools.partial(_flash_kernel, n_comp=n_comp, tk=tk, n_chunks=tkeys // tk,
                          has_extra=extra is not None, lam_init=lam_init),
        grid=(b, t // tq), in_specs=in_specs,
        out_specs=pl.BlockSpec((1, tq, QK_W), lambda bb, i: (bb, i, 0)),
        out_shape=jax.ShapeDtypeStruct((b, t, QK_W), BF16),
        scratch_shapes=[pltpu.VMEM((ng * tq, QK_W), BF16), pltpu.VMEM((ng, tq), F32),
                        pltpu.VMEM((ng * VT_ROWS, tq), F32),
                        pltpu.VMEM((tk, ng * tq), F32), pltpu.VMEM((tk, ng * tq), F32),
                        pltpu.VMEM((ng, tq), F32), pltpu.VMEM((ng, tq), F32)],
        compiler_params=_cparams(2),
        name="diff_attention" if diff is not None else "dense_attention",
    )(*args)


def _gla_dir(q_ref, k_ref, v_ref, g_ref, o_ref, state, cum_ref, attn_scr, o_scr, upd_scr, sin_scr, n_chunks, reverse):
    c = GLA_CHUNK
    tb = n_chunks * c
    b_all = _dot_f32_rhs(cum_ref[...], g_ref[0], 2)
    b3 = b_all.reshape(n_chunks, c, QK_W)
    btot3 = b3[:, 0:1] if reverse else b3[:, c - 1:c]
    bmid3 = b3[:, c // 2:c // 2 + 1]
    q3 = q_ref[0].reshape(n_chunks, c, QK_W)
    k3 = k_ref[0].reshape(n_chunks, c, QK_W)
    q_in = (q3 * jnp.exp(b3 - bmid3)).reshape(tb, QK_W).astype(BF16)
    k_in = (k3 * jnp.exp(bmid3 - b3)).reshape(tb, QK_W).astype(BF16)
    q_st = (q3 * jnp.exp(b3)).reshape(tb, QK_W).astype(BF16)
    k_st = (k3 * jnp.exp(btot3 - b3)).reshape(tb, QK_W)
    dec_all = jnp.exp(btot3)
    v = v_ref[0]

    nb_i = min(GLA_INTRA, tb)
    t = lax.broadcasted_iota(jnp.int32, (nb_i, nb_i), 0)
    s = lax.broadcasted_iota(jnp.int32, (nb_i, nb_i), 1)
    keep = ((t // c) == (s // c)) & ((s >= t) if reverse else (s <= t))
    zero = jnp.zeros((nb_i, QK_W), BF16)
    hm_i = [_head_mask((nb_i, QK_W), h * GLA_DK, (h + 1) * GLA_DK) for h in range(N_HEADS)]
    blocks = [(r0, h) for r0 in range(0, tb, nb_i) for h in range(N_HEADS)]
    for n, (r0, h) in enumerate(blocks):
        attn = lax.dot_general(q_in[r0:r0 + nb_i], jnp.where(hm_i[h], k_in[r0:r0 + nb_i], zero), _NT,
                               preferred_element_type=F32)
        attn_scr[n] = jnp.where(keep, attn, 0.0).astype(BF16)
    for n, (r0, h) in enumerate(blocks):
        vr = slice(h * GLA_DV, (h + 1) * GLA_DV)
        o_scr[r0:r0 + nb_i, vr] = jnp.dot(attn_scr[n], v[r0:r0 + nb_i, vr], preferred_element_type=F32)

    order = list(range(n_chunks - 1, -1, -1) if reverse else range(n_chunks))
    for ci in order:
        r0 = ci * c
        k_t = k_st[r0:r0 + c].T.astype(BF16)
        for h in range(N_HEADS):
            vr = slice(h * GLA_DV, (h + 1) * GLA_DV)
            upd_scr[ci, h * GLA_DK:(h + 1) * GLA_DK, :] = jnp.dot(
                k_t[h * GLA_DK:(h + 1) * GLA_DK], v[r0:r0 + c, vr], preferred_element_type=F32)
    s_cur = state[...]
    for ci in order:
        sin_scr[ci] = s_cur.astype(BF16)
        dec = jnp.broadcast_to(dec_all[ci], (GLA_DV, QK_W)).T
        s_cur = dec * s_cur + upd_scr[ci]
    state[...] = s_cur
    qzero = jnp.zeros((c, QK_W), BF16)
    hmasks = [_head_mask((c, QK_W), h * GLA_DK, (h + 1) * GLA_DK) for h in range(N_HEADS)]
    for ci in order:
        r0 = ci * c
        qc = q_st[r0:r0 + c]
        q_stack = jnp.concatenate([jnp.where(hm, qc, qzero) for hm in hmasks], axis=0)
        o_st = jnp.dot(q_stack, sin_scr[ci], preferred_element_type=F32)
        for h in range(N_HEADS):
            vr = slice(h * GLA_DV, (h + 1) * GLA_DV)
            o_ref[0, r0:r0 + c, vr] = (o_scr[r0:r0 + c, vr] + o_st[h * c:(h + 1) * c]).astype(o_ref.dtype)


def _gla_kernel(qf_ref, kf_ref, vf_ref, gf_ref, qb_ref, kb_ref, vb_ref, gb_ref, s0_ref,
                cumf_ref, cumb_ref, of_ref, ob_ref, sfin_ref, sf, sb, attn_f, attn_b, o_f, o_b,
                upd_f, upd_b, sin_f, sin_b, *, n_chunks):
    i = pl.program_id(1)

    @pl.when(i == 0)
    def _():
        sf[...] = s0_ref[0, 0]
        sb[...] = s0_ref[0, 1]

    _gla_dir(qf_ref, kf_ref, vf_ref, gf_ref, of_ref, sf, cumf_ref, attn_f, o_f, upd_f, sin_f, n_chunks, False)
    _gla_dir(qb_ref, kb_ref, vb_ref, gb_ref, ob_ref, sb, cumb_ref, attn_b, o_b, upd_b, sin_b, n_chunks, True)

    @pl.when(i == pl.num_programs(1) - 1)
    def _():
        sfin_ref[0, 0] = sf[...]
        sfin_ref[0, 1] = sb[...]


def _gla(p, s0, tb):
    b, t, _ = p["qc"].shape
    nb = t // tb
    n_chunks = tb // GLA_CHUNK
    nb_i = min(GLA_INTRA, tb)
    n_blk = (tb // nb_i) * N_HEADS
    idx = np.arange(tb)
    same = (idx[:, None] // GLA_CHUNK) == (idx[None, :] // GLA_CHUNK)
    cumf = jnp.asarray(same & (idx[None, :] <= idx[:, None]), BF16)
    cumb = jnp.asarray(same & (idx[None, :] >= idx[:, None]), BF16)
    fwd = lambda w: pl.BlockSpec((1, tb, w), lambda bb, i: (bb, i, 0))
    bwd = lambda w: pl.BlockSpec((1, tb, w), lambda bb, i: (bb, nb - 1 - i, 0))
    st = pl.BlockSpec((1, 2, QK_W, GLA_DV), lambda bb, i: (bb, 0, 0, 0))
    cm = pl.BlockSpec((tb, tb), lambda bb, i: (0, 0))
    sd = jax.ShapeDtypeStruct
    return pl.pallas_call(
        functools.partial(_gla_kernel, n_chunks=n_chunks),
        grid=(b, nb),
        in_specs=[fwd(QK_W), fwd(QK_W), fwd(GLA_V_W), fwd(QK_W),
                  bwd(QK_W), bwd(QK_W), bwd(GLA_V_W), bwd(QK_W), st, cm, cm],
        out_specs=[fwd(GLA_V_W), bwd(GLA_V_W), st],
        out_shape=[sd((b, t, GLA_V_W), BF16), sd((b, t, GLA_V_W), BF16), sd((b, 2, QK_W, GLA_DV), F32)],
        scratch_shapes=[pltpu.VMEM((QK_W, GLA_DV), F32), pltpu.VMEM((QK_W, GLA_DV), F32)]
        + [pltpu.VMEM((n_blk, nb_i, nb_i), BF16)] * 2 + [pltpu.VMEM((tb, GLA_V_W), F32)] * 2
        + [pltpu.VMEM((n_chunks, QK_W, GLA_DV), F32)] * 2 + [pltpu.VMEM((n_chunks, QK_W, GLA_DV), BF16)] * 2,
        compiler_params=_cparams(2), name="gla_scan",
    )(p["qc"], p["kc"], p["vc"], p["gf"], p["qc"], p["kc"], p["vc"], p["gb"], s0, cumf, cumb)


def _outproj_kernel(h_ref, oa_ref, ob_ref, ocf_ref, ocb_ref, gc_ref, on_ref, w_ref, g1_ref, o_ref):
    oc = ocf_ref[0].astype(F32) + ocb_ref[0].astype(F32)
    gate = gc_ref[0]
    parts = []
    for h in range(N_HEADS):
        vr = slice(h * GLA_DV, (h + 1) * GLA_DV)
        x = oc[:, vr]
        x = x * lax.rsqrt(jnp.mean(x * x, axis=-1, keepdims=True) + EPS) * on_ref[...]
        parts.append((x * _silu(gate[:, vr])).astype(BF16))
    y = (jnp.dot(oa_ref[0], w_ref[0:QK_W], preferred_element_type=F32)
         + jnp.dot(ob_ref[0], w_ref[QK_W:2 * QK_W], preferred_element_type=F32)
         + jnp.dot(jnp.concatenate(parts, axis=1), w_ref[2 * QK_W:], preferred_element_type=F32))
    o_ref[0] = h_ref[0] + g1_ref[0] * y


def _outproj(h, oa, ob, ocf, ocb, gc, onorm, w_out, g1, tm):
    b, t, d = h.shape
    tok = lambda w: pl.BlockSpec((1, tm, w), lambda bb, i: (bb, i, 0))
    return pl.pallas_call(
        _outproj_kernel, grid=(b, t // tm),
        in_specs=[tok(d), tok(QK_W), tok(QK_W), tok(GLA_V_W), tok(GLA_V_W), tok(GLA_V_W),
                  pl.BlockSpec((1, GLA_DV), lambda bb, i: (0, 0)),
                  pl.BlockSpec(w_out.shape, lambda bb, i: (0, 0)),
                  pl.BlockSpec((1, 1, d), lambda bb, i: (bb, 0, 0))],
        out_specs=tok(d), out_shape=jax.ShapeDtypeStruct((b, t, d), F32),
        compiler_params=_cparams(2), name="out_proj",
    )(h, oa, ob, ocf, ocb, gc, onorm, w_out, g1)


FFN_HALO = 16
FFN_CHUNKS = ((0, 1280), (1280, 1536))


def _ffn_kernel(h_ref, hp_ref, hn_ref, sc_ref, sh_ref, n2_ref, wg_ref, wu_ref, wd_ref, cw_ref, cb_ref,
                g2_ref, o_ref, xn_scr, a_scr, *, chunks):
    i = pl.program_id(1)
    nt = pl.num_programs(1)
    tm = h_ref.shape[1]
    n2 = n2_ref[...]
    sc = sc_ref[0]
    sh = sh_ref[0]

    def normed(x):
        y = x * lax.rsqrt(jnp.mean(x * x, axis=-1, keepdims=True) + EPS) * n2
        return y * (1.0 + sc) + sh

    x = h_ref[0]
    prev_ok = (i > 0).astype(F32)
    next_ok = (i < nt - 1).astype(F32)
    xn_scr[0:FFN_HALO] = (normed(hp_ref[0]) * prev_ok).astype(BF16)
    xn_scr[FFN_HALO:FFN_HALO + tm] = normed(x).astype(BF16)
    xn_scr[FFN_HALO + tm:] = (normed(hn_ref[0]) * next_ok).astype(BF16)

    acc = None
    for f0, fc in chunks:
        a_scr[:, :fc] = jnp.dot(xn_scr[...], wg_ref[:, f0:f0 + fc], preferred_element_type=F32)
        u = jnp.dot(xn_scr[FFN_HALO:FFN_HALO + tm], wu_ref[:, f0:f0 + fc], preferred_element_type=F32)
        cw = cw_ref[:, f0:f0 + fc]
        a = (cb_ref[:, f0:f0 + fc]
             + a_scr[FFN_HALO - 1:FFN_HALO - 1 + tm, :fc] * cw[0:1]
             + a_scr[FFN_HALO:FFN_HALO + tm, :fc] * cw[1:2]
             + a_scr[FFN_HALO + 1:FFN_HALO + 1 + tm, :fc] * cw[2:3])
        g = (_silu(a) * u).astype(BF16)
        t = jnp.dot(g, wd_ref[f0:f0 + fc], preferred_element_type=F32)
        acc = t if acc is None else acc + t
    o_ref[0] = x + g2_ref[0] * acc


def _ffn(h, sc, sh, n2, wg, wu, wd, cw, cb, g2, tm):
    b, t, d = h.shape
    nt = t // tm
    hb = tm // FFN_HALO
    tok = pl.BlockSpec((1, tm, d), lambda bb, i: (bb, i, 0))
    prev = pl.BlockSpec((1, FFN_HALO, d), lambda bb, i: (bb, jnp.maximum(i * hb - 1, 0), 0))
    nxt = pl.BlockSpec((1, FFN_HALO, d), lambda bb, i: (bb, jnp.minimum((i + 1) * hb, nt * hb - 1), 0))
    mod = pl.BlockSpec((1, 1, d), lambda bb, i: (bb, 0, 0))
    const = lambda shape: pl.BlockSpec(shape, lambda bb, i: (0,) * len(shape),
                                       pipeline_mode=pl.Buffered(1))
    return pl.pallas_call(
        functools.partial(_ffn_kernel, chunks=FFN_CHUNKS), grid=(b, nt),
        in_specs=[tok, prev, nxt, mod, mod, const((1, d)), const(wg.shape), const(wu.shape),
                  const(wd.shape), const(cw.shape), const(cb.shape), mod],
        out_specs=tok, out_shape=jax.ShapeDtypeStruct((b, t, d), F32),
        scratch_shapes=[pltpu.VMEM((tm + 2 * FFN_HALO, d), BF16),
                        pltpu.VMEM((tm + 2 * FFN_HALO, max(fc for _, fc in FFN_CHUNKS)), F32)],
        compiler_params=_cparams(2), name="conv_ffn",
    )(h, h, h, sc, sh, n2, wg, wu, wd, cw, cb, g2)


def _rope_tables(s):
    t = np.arange(s)
    row, col = t // GRID_W, t % GRID_W
    lane = np.arange(QK_W)
    jj = lane % DA_QK_DIM
    nf = DA_QK_DIM // 4
    inv = jnp.asarray(ROPE_THETA, F32) ** (-jnp.arange(nf, dtype=F32) / nf)
    pos = jnp.where((jj // (2 * nf) == 0)[None, :], row[:, None], col[:, None]).astype(F32)
    ang = pos * inv[jj % nf][None, :]
    first = ((jj % (2 * nf)) < nf)[None, :]
    cos, sin = jnp.cos(ang), jnp.sin(ang)
    return cos, jnp.where(first, -sin, 0.0), jnp.where(first, 0.0, sin)


def _block_ones(gsz):
    g = np.arange(QK_W) // gsz
    return jnp.asarray(g[:, None] == g[None, :], BF16)


def _layer_weights(l, w_in, qn_a, kn_a, qn_b, kn_b, w_a2_f, b_a_f, w_a2_b, b_a_b):
    w = w_in[l]
    d = w.shape[0]
    wn = jnp.concatenate([w[:, 0:512], w[:, 768:1280], w[:, 1536:3104],
                          jnp.zeros((d, GATE_PAD - 2 * GLA_GATE_RANK), F32)], axis=1).astype(BF16)
    wt = jnp.concatenate([w[:, 512:768], w[:, 1280:1536]], axis=1).T.astype(BF16)
    gains = jnp.stack([jnp.tile(qn_a[l], N_HEADS), jnp.tile(kn_a[l], N_HEADS),
                       jnp.tile(qn_b[l], 2 * N_HEADS), jnp.tile(kn_b[l], 2 * N_HEADS)])
    w2 = jnp.zeros((GATE_PAD, 2 * QK_W), F32)
    w2 = w2.at[0:GLA_GATE_RANK, 0:QK_W].set(w_a2_f[l])
    w2 = w2.at[GLA_GATE_RANK:2 * GLA_GATE_RANK, QK_W:].set(w_a2_b[l])
    w2_hi = w2.astype(BF16)
    w2_lo = (w2 - w2_hi.astype(F32)).astype(BF16)
    b2 = jnp.concatenate([b_a_f[l], b_a_b[l]])[None, :]
    return {"wn": wn, "wt": wt, "g64": _block_ones(HEAD_W), "g32": _block_ones(DA_QK_DIM),
            "gains": gains, "w2": jnp.stack([w2_hi, w2_lo]), "b2": b2}


def kernel(x, c, ctx, c_ctx, norm1, norm2, w_ada, b_ada, w_in, qn_a, kn_a, rpb_a, qn_b, kn_b,
           lam_q1, lam_k1, lam_q2, lam_k2, subln_b, w_a2_f, b_a_f, w_a2_b, b_a_b, onorm_c, w_out,
           w_g, w_u, conv_w, conv_b, w_d):
    bsz, s, d = x.shape
    lc = ctx.shape[1]
    depth = w_in.shape[0]
    rows = s // GRID_W
    tm = min(512, s)
    tb = min(512, s)

    cvec = jnp.zeros((16, d), F32).at[:bsz].set(c).at[bsz].set(c_ctx)
    ada = _ada(cvec, w_ada, b_ada)
    rope = _rope_tables(s)

    h, hc = x, ctx
    for l in range(depth):
        with_ctx_out = l < depth - 1
        lam_init = 0.8 - 0.6 * math.exp(-0.3 * l)
        m = ada[l, :bsz].reshape(bsz, 1, 6, d)
        mc = jnp.broadcast_to(ada[l, bsz].reshape(1, 1, 6, d), (bsz, 1, 6, d))
        sh1, sc1, g1, sh2, sc2, g2 = (m[:, :, j] for j in range(6))
        csh1, csc1, cg1, csh2, csc2, cg2 = (mc[:, :, j] for j in range(6))
        wts = _layer_weights(l, w_in, qn_a, kn_a, qn_b, kn_b, w_a2_f, b_a_f, w_a2_b, b_a_b)
        n1 = norm1[l][None, :]
        n2 = norm2[l][None, :]

        pl_ = _inproj(h, sc1, sh1, n1, wts, rope, tm)
        pc_ = _inproj(hc, csc1, csh1, n1, wts, None, lc)

        tab = _na_table(rpb_a[l], rows)
        o_a = _na_attention(pl_["qa"], pl_["ka"], pl_["vat"], pc_["ka"], pc_["vat"], tab)
        lamv = jnp.stack([lam_q1[l], lam_k1[l], lam_q2[l], lam_k2[l]])
        diff = (lamv, subln_b[l][:, None])
        o_b = _flash(pl_["qb"], pl_["kb"], pl_["vbt"], (pc_["kb"], pc_["vbt"]), diff, 256, 512, lam_init)

        s0 = jnp.zeros((bsz, 2, QK_W, GLA_DV), F32)
        ocf_c, ocb_c, s_ctx = _gla(pc_, s0, lc)
        ocf, ocb, _ = _gla(pl_, s_ctx, tb)

        w_o = w_out[l].astype(BF16)
        on = onorm_c[l][None, :]
        wg, wu, wd = w_g[l].astype(BF16), w_u[l].astype(BF16), w_d[l].astype(BF16)
        cw, cb = conv_w[l], conv_b[l][None, :]
        h = _outproj(h, o_a, o_b, ocf, ocb, pl_["gc"], on, w_o, g1, tm)
        h = _ffn(h, sc2, sh2, n2, wg, wu, wd, cw, cb, g2, tm)
        if with_ctx_out:
            o_a_c = _flash(pc_["qa"], pc_["ka"], pc_["vat"], None, None, lc, lc)
            o_b_c = _flash(pc_["qb"], pc_["kb"], pc_["vbt"], None, diff, lc, lc, lam_init)
            hc = _outproj(hc, o_a_c, o_b_c, ocf_c, ocb_c, pc_["gc"], on, w_o, cg1, lc)
            hc = _ffn(hc, csc2, csh2, n2, wg, wu, wd, cw, cb, cg2, lc)
    return h
```

```python
import functools
import math

import jax
import jax.numpy as jnp
import numpy as np
from jax import lax
from jax.experimental import pallas as pl
from jax.experimental.pallas import tpu as pltpu

F32 = jnp.float32
BF16 = jnp.bfloat16

D_MODEL = 1024
GRID_W = 64
HEAD_W = 64
N_HEADS = 4
NA_WIN_ROWS = 8
NA_WIN_COLS = 16
NA_Q_ROWS = 4
NA_K_ROWS = 12
DA_QK_DIM = 32
GLA_DK = 64
GLA_DV = 128
GLA_CHUNK = 64
GLA_INTRA = 128
GLA_GATE_RANK = 16
GLA_GATE_NORM = 16.0
GATE_PAD = 128
FFN_DIM = 2816
ROPE_THETA = 10000.0
EPS = 1e-6
NEG_INF = -1e30
LOG2E = math.log2(math.e)
QK_W = N_HEADS * HEAD_W
VT_ROWS = HEAD_W + 16
VT_W = N_HEADS * VT_ROWS
GLA_V_W = N_HEADS * GLA_DV
VMEM_LIMIT = 56 * 1024 * 1024

_NT = (((1,), (1,)), ((), ()))


def _cparams(n_axes):
    return pltpu.CompilerParams(dimension_semantics=("arbitrary",) * n_axes,
                                vmem_limit_bytes=VMEM_LIMIT)


def _split_bf16(x, parts):
    out = []
    r = x
    for _ in range(parts):
        p = r.astype(BF16)
        out.append(p)
        r = r - p.astype(F32)
    return out


def _dot_f32_lhs(x, m_bf16, parts):
    acc = None
    for p in _split_bf16(x, parts):
        t = jnp.dot(p, m_bf16, preferred_element_type=F32)
        acc = t if acc is None else acc + t
    return acc


def _dot_f32_rhs(m_bf16, x, parts):
    acc = None
    for p in _split_bf16(x, parts):
        t = jnp.dot(m_bf16, p, preferred_element_type=F32)
        acc = t if acc is None else acc + t
    return acc


def _silu(x):
    return x * (1.0 / (1.0 + jnp.exp(-x)))


def _log_sigmoid(x):
    return jnp.minimum(x, 0.0) - jnp.log(1.0 + jnp.exp(-jnp.abs(x)))


def _ada_kernel(c_ref, w_ref, b_ref, o_ref):
    s = _silu(c_ref[...])
    w = w_ref[0]
    acc = None
    for sp in _split_bf16(s, 3):
        for wp in _split_bf16(w, 2):
            t = jnp.dot(sp, wp, preferred_element_type=F32)
            acc = t if acc is None else acc + t
    o_ref[0] = acc + b_ref[0]


def _ada(cvec, w_ada, b_ada):
    n_l, d, n6 = w_ada.shape
    tn = 1536
    return pl.pallas_call(
        _ada_kernel,
        grid=(n_l, n6 // tn),
        in_specs=[pl.BlockSpec((16, d), lambda l, j: (0, 0)),
                  pl.BlockSpec((1, d, tn), lambda l, j: (l, 0, j)),
                  pl.BlockSpec((1, 1, tn), lambda l, j: (l, 0, j))],
        out_specs=pl.BlockSpec((1, 16, tn), lambda l, j: (l, 0, j)),
        out_shape=jax.ShapeDtypeStruct((n_l, 16, n6), F32),
        compiler_params=_cparams(2),
        name="ada_proj",
    )(cvec, w_ada, b_ada.reshape(n_l, 1, n6))


def _group_rms(x, gmat, gain, gsz):
    ssq = _dot_f32_lhs(x * x, gmat, 1)
    return x * lax.rsqrt(ssq * (1.0 / gsz) + EPS) * gain


def _inproj_kernel(h_ref, sc_ref, sh_ref, n1_ref, wn_ref, wt_ref, g64_ref, g32_ref, gains_ref,
                   w2_ref, b2_ref, *rest, rope):
    if rope:
        cos_ref, s1_ref, s2_ref = rest[:3]
        rest = rest[3:]
    (qa_ref, ka_ref, qb_ref, kb_ref, qc_ref, kc_ref, vc_ref, gc_ref, gf_ref, gb_ref,
     vat_ref, vbt_ref, y_scr) = rest

    x = h_ref[0]
    y = x * lax.rsqrt(jnp.mean(x * x, axis=-1, keepdims=True) + EPS) * n1_ref[...]
    xn = (y * (1.0 + sc_ref[0]) + sh_ref[0]).astype(BF16)

    y_scr[...] = jnp.dot(xn, wn_ref[...], preferred_element_type=F32)
    vt = lax.dot_general(wt_ref[...], xn, _NT, preferred_element_type=F32)

    def proj(lo, hi):
        return y_scr[:, lo:hi]

    g64 = g64_ref[...]
    g32 = g32_ref[...]
    gains = gains_ref[...]

    def rot(v):
        if not rope:
            return v
        return (v * cos_ref[...] + pltpu.roll(v, QK_W - 8, 1) * s1_ref[...]
                + pltpu.roll(v, 8, 1) * s2_ref[...])

    qa_ref[0] = (_group_rms(proj(0, 256), g64, gains[0:1], HEAD_W) * (HEAD_W ** -0.5 * LOG2E)).astype(BF16)
    ka_ref[0] = _group_rms(proj(256, 512), g64, gains[1:2], HEAD_W).astype(BF16)
    qb_ref[0] = (rot(_group_rms(proj(512, 768), g32, gains[2:3], DA_QK_DIM))
                 * (DA_QK_DIM ** -0.5 * LOG2E)).astype(BF16)
    kb_ref[0] = rot(_group_rms(proj(768, 1024), g32, gains[3:4], DA_QK_DIM)).astype(BF16)
    qc_ref[0] = proj(1024, 1280) * (GLA_DK ** -0.5)
    kc_ref[0] = proj(1280, 1536)
    vc_ref[0] = proj(1536, 2048).astype(BF16)
    gc_ref[0] = proj(2048, 2560)
    a_lr = proj(2560, 2560 + GATE_PAD)
    a_hi, a_lo = _split_bf16(a_lr, 2)
    pre = (jnp.dot(a_hi, w2_ref[0], preferred_element_type=F32)
           + jnp.dot(a_lo, w2_ref[0], preferred_element_type=F32)
           + jnp.dot(a_hi, w2_ref[1], preferred_element_type=F32))
    gate = _log_sigmoid(pre + b2_ref[...]) * (1.0 / GLA_GATE_NORM)
    gf_ref[0] = gate[:, :QK_W]
    gb_ref[0] = gate[:, QK_W:]
    ones = jnp.ones((VT_ROWS - HEAD_W, vt.shape[1]), BF16)
    for grp, ref in enumerate((vat_ref, vbt_ref)):
        for h in range(N_HEADS):
            src = (grp * N_HEADS + h) * HEAD_W
            ref[0, h * VT_ROWS:h * VT_ROWS + HEAD_W, :] = vt[src:src + HEAD_W].astype(BF16)
            ref[0, h * VT_ROWS + HEAD_W:(h + 1) * VT_ROWS, :] = ones


def _inproj(h, sc, sh, n1, wts, rope_tabs, tm):
    b, t, d = h.shape
    nt = t // tm
    rope = rope_tabs is not None
    const = lambda shape: pl.BlockSpec(shape, lambda i, bb: (0,) * len(shape))
    tok = lambda w: pl.BlockSpec((1, tm, w), lambda i, bb: (bb, i, 0))
    mod = pl.BlockSpec((1, 1, d), lambda i, bb: (bb, 0, 0))
    in_specs = [tok(d), mod, mod, const((1, d)), const(wts["wn"].shape), const(wts["wt"].shape),
                const((QK_W, QK_W)), const((QK_W, QK_W)), const((4, QK_W)),
                const((2, GATE_PAD, 2 * QK_W)), const((1, 2 * QK_W))]
    args = [h, sc, sh, n1, wts["wn"], wts["wt"], wts["g64"], wts["g32"], wts["gains"],
            wts["w2"], wts["b2"]]
    if rope:
        in_specs += [pl.BlockSpec((tm, QK_W), lambda i, bb: (i, 0))] * 3
        args += list(rope_tabs)
    tspec = pl.BlockSpec((1, VT_W, tm), lambda i, bb: (bb, 0, i))
    out_specs = [tok(QK_W)] * 6 + [tok(GLA_V_W), tok(GLA_V_W), tok(QK_W), tok(QK_W), tspec, tspec]
    sd = jax.ShapeDtypeStruct
    out_shape = [sd((b, t, QK_W), BF16)] * 4 + [sd((b, t, QK_W), F32)] * 2 + [
        sd((b, t, GLA_V_W), BF16), sd((b, t, GLA_V_W), F32), sd((b, t, QK_W), F32),
        sd((b, t, QK_W), F32), sd((b, VT_W, t), BF16), sd((b, VT_W, t), BF16)]
    outs = pl.pallas_call(
        functools.partial(_inproj_kernel, rope=rope),
        grid=(nt, b), in_specs=in_specs, out_specs=out_specs, out_shape=out_shape,
        scratch_shapes=[pltpu.VMEM((tm, wts["wn"].shape[1]), F32)],
        compiler_params=_cparams(2), name="in_proj_rope" if rope else "in_proj",
    )(*args)
    names = ("qa", "ka", "qb", "kb", "qc", "kc", "vc", "gc", "gf", "gb", "vat", "vbt")
    return dict(zip(names, outs))


def _head_mask(shape, lo, hi):
    lane = lax.broadcasted_iota(jnp.int32, shape, 1)
    return (lane >= lo) & (lane < hi)


def _na_kernel(q_ref, k_ref, vt_ref, kc_ref, vct_ref, tab_ref, o_ref, qm_scr, s_scr, *, n_rows):
    i = pl.program_id(1)
    nq = NA_Q_ROWS * GRID_W
    nk = NA_K_ROWS * GRID_W
    base_row = jnp.clip(NA_Q_ROWS * i - NA_WIN_ROWS // 2, 0, n_rows - NA_K_ROWS)
    base = pl.multiple_of(base_row * GRID_W, 256)
    btype = jnp.where(i == 0, 0, jnp.where(i == pl.num_programs(1) - 1, 2, 1))
    kw = k_ref[0, pl.ds(base, nk), :]
    vw = vt_ref[0, :, pl.ds(base, nk)]
    kc = kc_ref[0]
    vct = vct_ref[0]
    q = q_ref[0]
    zero = jnp.zeros_like(q)
    for h in range(N_HEADS):
        qm_scr[h * nq:(h + 1) * nq, :] = jnp.where(_head_mask(q.shape, h * HEAD_W, (h + 1) * HEAD_W), q, zero)
    s_scr[...] = lax.dot_general(jnp.concatenate([kw, kc], axis=0), qm_scr[...], _NT,
                                 preferred_element_type=F32)
    vt_all = jnp.concatenate([vw, vct], axis=1)
    for h in range(N_HEADS):
        cols = slice(h * nq, (h + 1) * nq)
        sw = s_scr[0:nk, cols] + tab_ref[btype, h]
        sc = s_scr[nk:, cols]
        m = jnp.maximum(jnp.max(sw, axis=0, keepdims=True), jnp.max(sc, axis=0, keepdims=True))
        p = jnp.concatenate([jnp.exp2(sw - m), jnp.exp2(sc - m)], axis=0).astype(BF16)
        ot = jnp.dot(vt_all[h * VT_ROWS:(h + 1) * VT_ROWS], p, preferred_element_type=F32)
        o_ref[0, :, h * HEAD_W:(h + 1) * HEAD_W] = (ot[:HEAD_W] / ot[HEAD_W:HEAD_W + 1]).T.astype(BF16)


RPB_ROWS = 2 * NA_WIN_ROWS - 1
RPB_COLS = 2 * NA_WIN_COLS - 1


def _na_table_kernel(rpb_ref, o_ref, *, n_rows):
    h = pl.program_id(0)
    c = lax.broadcasted_iota(jnp.int32, (GRID_W, GRID_W), 0)
    w = lax.broadcasted_iota(jnp.int32, (GRID_W, GRID_W), 1)
    co = jnp.clip(c - w, -(NA_WIN_COLS - 1), NA_WIN_COLS - 1) + NA_WIN_COLS - 1
    c0 = jnp.clip(w - NA_WIN_COLS // 2, 0, GRID_W - NA_WIN_COLS)
    col_ok = (c >= c0) & (c < c0 + NA_WIN_COLS)
    tiles = []
    for ro in range(RPB_ROWS):
        t = jnp.zeros((GRID_W, GRID_W), F32)
        for kk in range(RPB_COLS):
            t = jnp.where(co == kk, rpb_ref[(h * RPB_ROWS + ro) * RPB_COLS + kk], t)
        tiles.append(jnp.where(col_ok, t * LOG2E, NEG_INF))
    outside = jnp.full((GRID_W, GRID_W), NEG_INF, F32)
    n_blocks = n_rows // NA_Q_ROWS
    for bt, blk in enumerate((0, 1, n_blocks - 1)):
        base_row = min(max(NA_Q_ROWS * blk - NA_WIN_ROWS // 2, 0), n_rows - NA_K_ROWS)
        for a in range(NA_K_ROWS):
            for b in range(NA_Q_ROWS):
                key_row, q_row = base_row + a, NA_Q_ROWS * blk + b
                r0 = min(max(q_row - NA_WIN_ROWS // 2, 0), n_rows - NA_WIN_ROWS)
                inside = r0 <= key_row < r0 + NA_WIN_ROWS
                o_ref[bt, 0, a * GRID_W:(a + 1) * GRID_W, b * GRID_W:(b + 1) * GRID_W] = (
                    tiles[key_row - q_row + NA_WIN_ROWS - 1] if inside else outside)


def _na_table(rpb, n_rows):
    n_h = rpb.shape[0]
    nk, nq = NA_K_ROWS * GRID_W, NA_Q_ROWS * GRID_W
    flat = jnp.zeros((2048,), F32).at[:n_h * RPB_ROWS * RPB_COLS].set(rpb.reshape(-1))
    return pl.pallas_call(
        functools.partial(_na_table_kernel, n_rows=n_rows), grid=(n_h,),
        in_specs=[pl.BlockSpec(memory_space=pltpu.SMEM)],
        out_specs=pl.BlockSpec((3, 1, nk, nq), lambda h: (0, h, 0, 0)),
        out_shape=jax.ShapeDtypeStruct((3, n_h, nk, nq), F32),
        compiler_params=_cparams(1), name="na_bias_table",
    )(flat)


def _na_attention(q, k, vt, kc, vct, tab):
    b, s, _ = q.shape
    rows = s // GRID_W
    nb = rows // NA_Q_ROWS
    nq = NA_Q_ROWS * GRID_W
    lc = kc.shape[1]
    return pl.pallas_call(
        functools.partial(_na_kernel, n_rows=rows),
        grid=(b, nb),
        in_specs=[pl.BlockSpec((1, nq, QK_W), lambda bb, i: (bb, i, 0)),
                  pl.BlockSpec((1, s, QK_W), lambda bb, i: (bb, 0, 0)),
                  pl.BlockSpec((1, VT_W, s), lambda bb, i: (bb, 0, 0)),
                  pl.BlockSpec((1, lc, QK_W), lambda bb, i: (bb, 0, 0)),
                  pl.BlockSpec((1, VT_W, lc), lambda bb, i: (bb, 0, 0)),
                  pl.BlockSpec(tab.shape, lambda bb, i: (0, 0, 0, 0), pipeline_mode=pl.Buffered(1))],
        out_specs=pl.BlockSpec((1, nq, QK_W), lambda bb, i: (bb, i, 0)),
        out_shape=jax.ShapeDtypeStruct((b, s, QK_W), BF16),
        scratch_shapes=[pltpu.VMEM((N_HEADS * nq, QK_W), BF16),
                        pltpu.VMEM((NA_K_ROWS * GRID_W + lc, N_HEADS * nq), F32)],
        compiler_params=_cparams(2), name="na_attention",
    )(q, k, vt, kc, vct, tab)


def _flash_kernel(*refs, n_comp, tk, n_chunks, has_extra, lam_init):
    q_ref, k_ref, vt_ref = refs[:3]
    refs = refs[3:]
    if has_extra:
        kx_ref, vxt_ref = refs[:2]
        refs = refs[2:]
    if n_comp == 2:
        lam_ref, gain_ref = refs[:2]
        refs = refs[2:]
    o_ref, qm_scr, m_scr, acc_scr, s0_scr, s1_scr, c0_scr, c1_scr = refs

    q = q_ref[0]
    tq = q.shape[0]
    dsub = HEAD_W // n_comp
    ng = N_HEADS * n_comp
    zero = jnp.zeros_like(q)
    for g in range(ng):
        qm_scr[g * tq:(g + 1) * tq, :] = jnp.where(_head_mask(q.shape, g * dsub, (g + 1) * dsub), q, zero)
    m_scr[...] = jnp.full(m_scr.shape, NEG_INF, F32)
    acc_scr[...] = jnp.zeros(acc_scr.shape, F32)

    def issue_scores(dst, cmax, kc):
        nkeys = kc.shape[0]

        def issue(g):
            sg = lax.dot_general(kc, qm_scr[g * tq:(g + 1) * tq, :], _NT, preferred_element_type=F32)
            dst[0:nkeys, g * tq:(g + 1) * tq] = sg
            cmax[g:g + 1] = jnp.max(sg, axis=0, keepdims=True)
        return issue

    def softmax_pv(src, cmax, vtc, before_group=None):
        nkeys = vtc.shape[1]
        for g in range(ng):
            if before_group is not None:
                before_group(g)
            h = g // n_comp
            m_old = m_scr[g:g + 1]
            m_new = jnp.maximum(m_old, cmax[g:g + 1])
            alpha = jnp.exp2(m_old - m_new)
            p = jnp.exp2(src[0:nkeys, g * tq:(g + 1) * tq] - m_new)
            rows = slice(g * VT_ROWS, (g + 1) * VT_ROWS)
            acc_scr[rows] = alpha * acc_scr[rows] + jnp.dot(
                vtc[h * VT_ROWS:(h + 1) * VT_ROWS], p.astype(BF16), preferred_element_type=F32)
            m_scr[g:g + 1] = m_new

    def keys(j):
        return k_ref[0, pl.ds(pl.multiple_of(j * tk, tk), tk), :]

    def values_t(j):
        return vt_ref[0, :, pl.ds(pl.multiple_of(j * tk, tk), tk)]

    first = issue_scores(s0_scr, c0_scr, keys(0))
    for g in range(ng):
        first(g)
    if n_chunks > 1:
        def pair(j, last):
            softmax_pv(s0_scr, c0_scr, values_t(j), issue_scores(s1_scr, c1_scr, keys(j + 1)))
            if not last:
                nxt = issue_scores(s0_scr, c0_scr, keys(j + 2))
            elif has_extra:
                nxt = issue_scores(s0_scr, c0_scr, kx_ref[0])
            else:
                nxt = None
            softmax_pv(s1_scr, c1_scr, values_t(j + 1), nxt)

        def body(i, carry):
            pair(2 * i, False)
            return carry

        lax.fori_loop(0, n_chunks // 2 - 1, body, 0)
        pair(n_chunks - 2, True)
        if has_extra:
            softmax_pv(s0_scr, c0_scr, vxt_ref[0])
    else:
        nxt = issue_scores(s1_scr, c1_scr, kx_ref[0]) if has_extra else None
        softmax_pv(s0_scr, c0_scr, values_t(0), nxt)
        if has_extra:
            softmax_pv(s1_scr, c1_scr, vxt_ref[0])

    if n_comp == 2:
        lv = lam_ref[...]
        lam = (jnp.exp(jnp.sum(lv[0:1] * lv[1:2], axis=1, keepdims=True))
               - jnp.exp(jnp.sum(lv[2:3] * lv[3:4], axis=1, keepdims=True)) + lam_init)
    for h in range(N_HEADS):
        cols = slice(h * HEAD_W, (h + 1) * HEAD_W)
        if n_comp == 2:
            a0 = acc_scr[2 * h * VT_ROWS:(2 * h + 1) * VT_ROWS]
            a1 = acc_scr[(2 * h + 1) * VT_ROWS:(2 * h + 2) * VT_ROWS]
            o = (a0[:HEAD_W] / a0[HEAD_W:HEAD_W + 1]
                 - lam * (a1[:HEAD_W] / a1[HEAD_W:HEAD_W + 1]))
            o = (o * lax.rsqrt(jnp.mean(o * o, axis=0, keepdims=True) + EPS)
                 * gain_ref[...] * (1.0 - lam_init))
        else:
            a0 = acc_scr[h * VT_ROWS:(h + 1) * VT_ROWS]
            o = a0[:HEAD_W] / a0[HEAD_W:HEAD_W + 1]
        o_ref[0, :, cols] = o.T.astype(BF16)


def _flash(q, k, vt, extra, diff, tq, tk, lam_init=0.0):
    b, t, _ = q.shape
    tkeys = k.shape[1]
    n_comp = 2 if diff is not None else 1
    ng = N_HEADS * n_comp
    in_specs = [pl.BlockSpec((1, tq, QK_W), lambda bb, i: (bb, i, 0)),
                pl.BlockSpec((1, tkeys, QK_W), lambda bb, i: (bb, 0, 0)),
                pl.BlockSpec((1, VT_W, tkeys), lambda bb, i: (bb, 0, 0))]
    args = [q, k, vt]
    if extra is not None:
        lx = extra[0].shape[1]
        in_specs += [pl.BlockSpec((1, lx, QK_W), lambda bb, i: (bb, 0, 0)),
                     pl.BlockSpec((1, VT_W, lx), lambda bb, i: (bb, 0, 0))]
        args += list(extra)
    if diff is not None:
        in_specs += [pl.BlockSpec((4, DA_QK_DIM), lambda bb, i: (0, 0)),
                     pl.BlockSpec((HEAD_W, 1), lambda bb, i: (0, 0))]
        args += list(diff)
    return pl.pallas_call(
        functools.partial(_flash_kernel, n_comp=n_comp, tk=tk, n_chunks=tkeys // tk,
                          has_extra=extra is not None, lam_init=lam_init),
        grid=(b, t // tq), in_specs=in_specs,
        out_specs=pl.BlockSpec((1, tq, QK_W), lambda bb, i: (bb, i, 0)),
        out_shape=jax.ShapeDtypeStruct((b, t, QK_W), BF16),
        scratch_shapes=[pltpu.VMEM((ng * tq, QK_W), BF16), pltpu.VMEM((ng, tq), F32),
                        pltpu.VMEM((ng * VT_ROWS, tq), F32),
                        pltpu.VMEM((tk, ng * tq), F32), pltpu.VMEM((tk, ng * tq), F32),
                        pltpu.VMEM((ng, tq), F32), pltpu.VMEM((ng, tq), F32)],
        compiler_params=_cparams(2),
        name="diff_attention" if diff is not None else "dense_attention",
    )(*args)


def _gla_dir(q_ref, k_ref, v_ref, g_ref, o_ref, state, cum_ref, attn_scr, o_scr, upd_scr, sin_scr, n_chunks, reverse):
    c = GLA_CHUNK
    tb = n_chunks * c
    b_all = _dot_f32_rhs(cum_ref[...], g_ref[0], 2)
    b3 = b_all.reshape(n_chunks, c, QK_W)
    btot3 = b3[:, 0:1] if reverse else b3[:, c - 1:c]
    bmid3 = b3[:, c // 2:c // 2 + 1]
    q3 = q_ref[0].reshape(n_chunks, c, QK_W)
    k3 = k_ref[0].reshape(n_chunks, c, QK_W)
    q_in = (q3 * jnp.exp(b3 - bmid3)).reshape(tb, QK_W).astype(BF16)
    k_in = (k3 * jnp.exp(bmid3 - b3)).reshape(tb, QK_W).astype(BF16)
    q_st = (q3 * jnp.exp(b3)).reshape(tb, QK_W).astype(BF16)
    k_st = (k3 * jnp.exp(btot3 - b3)).reshape(tb, QK_W)
    dec_all = jnp.exp(btot3)
    v = v_ref[0]

    nb_i = min(GLA_INTRA, tb)
    t = lax.broadcasted_iota(jnp.int32, (nb_i, nb_i), 0)
    s = lax.broadcasted_iota(jnp.int32, (nb_i, nb_i), 1)
    keep = ((t // c) == (s // c)) & ((s >= t) if reverse else (s <= t))
    zero = jnp.zeros((nb_i, QK_W), BF16)
    hm_i = [_head_mask((nb_i, QK_W), h * GLA_DK, (h + 1) * GLA_DK) for h in range(N_HEADS)]
    blocks = [(r0, h) for r0 in range(0, tb, nb_i) for h in range(N_HEADS)]
    for n, (r0, h) in enumerate(blocks):
        attn = lax.dot_general(q_in[r0:r0 + nb_i], jnp.where(hm_i[h], k_in[r0:r0 + nb_i], zero), _NT,
                               preferred_element_type=F32)
        attn_scr[n] = jnp.where(keep, attn, 0.0).astype(BF16)
    for n, (r0, h) in enumerate(blocks):
        vr = slice(h * GLA_DV, (h + 1) * GLA_DV)
        o_scr[r0:r0 + nb_i, vr] = jnp.dot(attn_scr[n], v[r0:r0 + nb_i, vr], preferred_element_type=F32)

    order = list(range(n_chunks - 1, -1, -1) if reverse else range(n_chunks))
    for ci in order:
        r0 = ci * c
        k_t = k_st[r0:r0 + c].T.astype(BF16)
        for h in range(N_HEADS):
            vr = slice(h * GLA_DV, (h + 1) * GLA_DV)
            upd_scr[ci, h * GLA_DK:(h + 1) * GLA_DK, :] = jnp.dot(
                k_t[h * GLA_DK:(h + 1) * GLA_DK], v[r0:r0 + c, vr], preferred_element_type=F32)
    s_cur = state[...]
    for ci in order:
        sin_scr[ci] = s_cur.astype(BF16)
        dec = jnp.broadcast_to(dec_all[ci], (GLA_DV, QK_W)).T
        s_cur = dec * s_cur + upd_scr[ci]
    state[...] = s_cur
    qzero = jnp.zeros((c, QK_W), BF16)
    hmasks = [_head_mask((c, QK_W), h * GLA_DK, (h + 1) * GLA_DK) for h in range(N_HEADS)]
    for ci in order:
        r0 = ci * c
        qc = q_st[r0:r0 + c]
        q_stack = jnp.concatenate([jnp.where(hm, qc, qzero) for hm in hmasks], axis=0)
        o_st = jnp.dot(q_stack, sin_scr[ci], preferred_element_type=F32)
        for h in range(N_HEADS):
            vr = slice(h * GLA_DV, (h + 1) * GLA_DV)
            o_ref[0, r0:r0 + c, vr] = (o_scr[r0:r0 + c, vr] + o_st[h * c:(h + 1) * c]).astype(o_ref.dtype)


def _gla_kernel(qf_ref, kf_ref, vf_ref, gf_ref, qb_ref, kb_ref, vb_ref, gb_ref, s0_ref,
                cumf_ref, cumb_ref, of_ref, ob_ref, sfin_ref, sf, sb, attn_f, attn_b, o_f, o_b,
                upd_f, upd_b, sin_f, sin_b, *, n_chunks):
    i = pl.program_id(1)

    @pl.when(i == 0)
    def _():
        sf[...] = s0_ref[0, 0]
        sb[...] = s0_ref[0, 1]

    _gla_dir(qf_ref, kf_ref, vf_ref, gf_ref, of_ref, sf, cumf_ref, attn_f, o_f, upd_f, sin_f, n_chunks, False)
    _gla_dir(qb_ref, kb_ref, vb_ref, gb_ref, ob_ref, sb, cumb_ref, attn_b, o_b, upd_b, sin_b, n_chunks, True)

    @pl.when(i == pl.num_programs(1) - 1)
    def _():
        sfin_ref[0, 0] = sf[...]
        sfin_ref[0, 1] = sb[...]


def _gla(p, s0, tb):
    b, t, _ = p["qc"].shape
    nb = t // tb
    n_chunks = tb // GLA_CHUNK
    nb_i = min(GLA_INTRA, tb)
    n_blk = (tb // nb_i) * N_HEADS
    idx = np.arange(tb)
    same = (idx[:, None] // GLA_CHUNK) == (idx[None, :] // GLA_CHUNK)
    cumf = jnp.asarray(same & (idx[None, :] <= idx[:, None]), BF16)
    cumb = jnp.asarray(same & (idx[None, :] >= idx[:, None]), BF16)
    fwd = lambda w: pl.BlockSpec((1, tb, w), lambda bb, i: (bb, i, 0))
    bwd = lambda w: pl.BlockSpec((1, tb, w), lambda bb, i: (bb, nb - 1 - i, 0))
    st = pl.BlockSpec((1, 2, QK_W, GLA_DV), lambda bb, i: (bb, 0, 0, 0))
    cm = pl.BlockSpec((tb, tb), lambda bb, i: (0, 0))
    sd = jax.ShapeDtypeStruct
    return pl.pallas_call(
        functools.partial(_gla_kernel, n_chunks=n_chunks),
        grid=(b, nb),
        in_specs=[fwd(QK_W), fwd(QK_W), fwd(GLA_V_W), fwd(QK_W),
                  bwd(QK_W), bwd(QK_W), bwd(GLA_V_W), bwd(QK_W), st, cm, cm],
        out_specs=[fwd(GLA_V_W), bwd(GLA_V_W), st],
        out_shape=[sd((b, t, GLA_V_W), BF16), sd((b, t, GLA_V_W), BF16), sd((b, 2, QK_W, GLA_DV), F32)],
        scratch_shapes=[pltpu.VMEM((QK_W, GLA_DV), F32), pltpu.VMEM((QK_W, GLA_DV), F32)]
        + [pltpu.VMEM((n_blk, nb_i, nb_i), BF16)] * 2 + [pltpu.VMEM((tb, GLA_V_W), F32)] * 2
        + [pltpu.VMEM((n_chunks, QK_W, GLA_DV), F32)] * 2 + [pltpu.VMEM((n_chunks, QK_W, GLA_DV), BF16)] * 2,
        compiler_params=_cparams(2), name="gla_scan",
    )(p["qc"], p["kc"], p["vc"], p["gf"], p["qc"], p["kc"], p["vc"], p["gb"], s0, cumf, cumb)


FFN_HALO = 16
FFN_CHUNKS = ((0, 1280), (1280, 1536))
N_MIX = 5


def _mix_rows(oa, ob, ocf, ocb, gc, onorm):
    oc = ocf.astype(F32) + ocb.astype(F32)
    parts = [oa, ob]
    for h in range(N_HEADS):
        vr = slice(h * GLA_DV, (h + 1) * GLA_DV)
        x = oc[:, vr]
        x = x * lax.rsqrt(jnp.mean(x * x, axis=-1, keepdims=True) + EPS) * onorm
        parts.append((x * _silu(gc[:, vr])).astype(BF16))
    return jnp.concatenate(parts, axis=1)


def _ffn_kernel(*refs, chunks):
    h_ref, hp_ref, hn_ref = refs[0:3]
    mix_main, mix_prev, mix_next = (refs[3 + N_MIX * j:3 + N_MIX * (j + 1)] for j in range(3))
    (on_ref, wo_ref, g1_ref, sc_ref, sh_ref, n2_ref, wg_ref, wu_ref, wd_ref, cw_ref, cb_ref, g2_ref,
     o_ref, mix_scr, xn_scr, a_scr) = refs[3 + 3 * N_MIX:]
    i = pl.program_id(1)
    nt = pl.num_programs(1)
    tm = h_ref.shape[1]
    lo, hi = FFN_HALO, FFN_HALO + tm
    n2 = n2_ref[...]
    sc = sc_ref[0]
    sh = sh_ref[0]
    g1 = g1_ref[0]
    onorm = on_ref[...]

    mix_scr[0:lo] = _mix_rows(*(r[0] for r in mix_prev), onorm)
    mix_scr[lo:hi] = _mix_rows(*(r[0] for r in mix_main), onorm)
    mix_scr[hi:] = _mix_rows(*(r[0] for r in mix_next), onorm)
    y = jnp.dot(mix_scr[...], wo_ref[...], preferred_element_type=F32)

    def normed(x):
        z = x * lax.rsqrt(jnp.mean(x * x, axis=-1, keepdims=True) + EPS) * n2
        return z * (1.0 + sc) + sh

    x = h_ref[0] + g1 * y[lo:hi]
    prev_ok = (i > 0).astype(F32)
    next_ok = (i < nt - 1).astype(F32)
    xn_scr[0:lo] = (normed(hp_ref[0] + g1 * y[0:lo]) * prev_ok).astype(BF16)
    xn_scr[lo:hi] = normed(x).astype(BF16)
    xn_scr[hi:] = (normed(hn_ref[0] + g1 * y[hi:]) * next_ok).astype(BF16)

    acc = None
    for f0, fc in chunks:
        a_scr[:, :fc] = jnp.dot(xn_scr[...], wg_ref[:, f0:f0 + fc], preferred_element_type=F32)
        u = jnp.dot(xn_scr[lo:hi], wu_ref[:, f0:f0 + fc], preferred_element_type=F32)
        cw = cw_ref[:, f0:f0 + fc]
        a = (cb_ref[:, f0:f0 + fc]
             + a_scr[lo - 1:hi - 1, :fc] * cw[0:1]
             + a_scr[lo:hi, :fc] * cw[1:2]
             + a_scr[lo + 1:hi + 1, :fc] * cw[2:3])
        g = (_silu(a) * u).astype(BF16)
        t = jnp.dot(g, wd_ref[f0:f0 + fc], preferred_element_type=F32)
        acc = t if acc is None else acc + t
    o_ref[0] = x + g2_ref[0] * acc


def _mix_ffn(h, mix, onorm, w_out, g1, sc, sh, n2, wg, wu, wd, cw, cb, g2, tm):
    b, t, d = h.shape
    nt = t // tm
    hb = tm // FFN_HALO
    tok = lambda w: pl.BlockSpec((1, tm, w), lambda bb, i: (bb, i, 0))
    prev = lambda w: pl.BlockSpec((1, FFN_HALO, w), lambda bb, i: (bb, jnp.maximum(i * hb - 1, 0), 0))
    nxt = lambda w: pl.BlockSpec((1, FFN_HALO, w), lambda bb, i: (bb, jnp.minimum((i + 1) * hb, nt * hb - 1), 0))
    mod = pl.BlockSpec((1, 1, d), lambda bb, i: (bb, 0, 0))
    const = lambda shape: pl.BlockSpec(shape, lambda bb, i: (0,) * len(shape),
                                       pipeline_mode=pl.Buffered(1))
    widths = [m.shape[-1] for m in mix]
    in_specs = ([tok(d), prev(d), nxt(d)] + [tok(w) for w in widths] + [prev(w) for w in widths]
                + [nxt(w) for w in widths]
                + [const((1, GLA_DV)), const(w_out.shape), mod, mod, mod, const((1, d)), const(wg.shape),
                   const(wu.shape), const(wd.shape), const(cw.shape), const(cb.shape), mod])
    return pl.pallas_call(
        functools.partial(_ffn_kernel, chunks=FFN_CHUNKS), grid=(b, nt),
        in_specs=in_specs,
        out_specs=tok(d), out_shape=jax.ShapeDtypeStruct((b, t, d), F32),
        scratch_shapes=[pltpu.VMEM((tm + 2 * FFN_HALO, d), BF16),
                        pltpu.VMEM((tm + 2 * FFN_HALO, d), BF16),
                        pltpu.VMEM((tm + 2 * FFN_HALO, max(fc for _, fc in FFN_CHUNKS)), F32)],
        compiler_params=_cparams(2), name="mix_ffn",
    )(h, h, h, *mix, *mix, *mix, onorm, w_out, g1, sc, sh, n2, wg, wu, wd, cw, cb, g2)


def _rope_tables(s):
    t = np.arange(s)
    row, col = t // GRID_W, t % GRID_W
    lane = np.arange(QK_W)
    jj = lane % DA_QK_DIM
    nf = DA_QK_DIM // 4
    inv = jnp.asarray(ROPE_THETA, F32) ** (-jnp.arange(nf, dtype=F32) / nf)
    pos = jnp.where((jj // (2 * nf) == 0)[None, :], row[:, None], col[:, None]).astype(F32)
    ang = pos * inv[jj % nf][None, :]
    first = ((jj % (2 * nf)) < nf)[None, :]
    cos, sin = jnp.cos(ang), jnp.sin(ang)
    return cos, jnp.where(first, -sin, 0.0), jnp.where(first, 0.0, sin)


def _block_ones(gsz):
    g = np.arange(QK_W) // gsz
    return jnp.asarray(g[:, None] == g[None, :], BF16)


def _layer_weights(l, w_in, qn_a, kn_a, qn_b, kn_b, w_a2_f, b_a_f, w_a2_b, b_a_b):
    w = w_in[l]
    d = w.shape[0]
    wn = jnp.concatenate([w[:, 0:512], w[:, 768:1280], w[:, 1536:3104],
                          jnp.zeros((d, GATE_PAD - 2 * GLA_GATE_RANK), F32)], axis=1).astype(BF16)
    wt = jnp.concatenate([w[:, 512:768], w[:, 1280:1536]], axis=1).T.astype(BF16)
    gains = jnp.stack([jnp.tile(qn_a[l], N_HEADS), jnp.tile(kn_a[l], N_HEADS),
                       jnp.tile(qn_b[l], 2 * N_HEADS), jnp.tile(kn_b[l], 2 * N_HEADS)])
    w2 = jnp.zeros((GATE_PAD, 2 * QK_W), F32)
    w2 = w2.at[0:GLA_GATE_RANK, 0:QK_W].set(w_a2_f[l])
    w2 = w2.at[GLA_GATE_RANK:2 * GLA_GATE_RANK, QK_W:].set(w_a2_b[l])
    w2_hi = w2.astype(BF16)
    w2_lo = (w2 - w2_hi.astype(F32)).astype(BF16)
    b2 = jnp.concatenate([b_a_f[l], b_a_b[l]])[None, :]
    return {"wn": wn, "wt": wt, "g64": _block_ones(HEAD_W), "g32": _block_ones(DA_QK_DIM),
            "gains": gains, "w2": jnp.stack([w2_hi, w2_lo]), "b2": b2}


def kernel(x, c, ctx, c_ctx, norm1, norm2, w_ada, b_ada, w_in, qn_a, kn_a, rpb_a, qn_b, kn_b,
           lam_q1, lam_k1, lam_q2, lam_k2, subln_b, w_a2_f, b_a_f, w_a2_b, b_a_b, onorm_c, w_out,
           w_g, w_u, conv_w, conv_b, w_d):
    bsz, s, d = x.shape
    lc = ctx.shape[1]
    depth = w_in.shape[0]
    rows = s // GRID_W
    tm = min(512, s)
    tb = min(512, s)

    cvec = jnp.zeros((16, d), F32).at[:bsz].set(c).at[bsz].set(c_ctx)
    ada = _ada(cvec, w_ada, b_ada)
    rope = _rope_tables(s)

    h, hc = x, ctx
    for l in range(depth):
        with_ctx_out = l < depth - 1
        lam_init = 0.8 - 0.6 * math.exp(-0.3 * l)
        m = ada[l, :bsz].reshape(bsz, 1, 6, d)
        mc = jnp.broadcast_to(ada[l, bsz].reshape(1, 1, 6, d), (bsz, 1, 6, d))
        sh1, sc1, g1, sh2, sc2, g2 = (m[:, :, j] for j in range(6))
        csh1, csc1, cg1, csh2, csc2, cg2 = (mc[:, :, j] for j in range(6))
        wts = _layer_weights(l, w_in, qn_a, kn_a, qn_b, kn_b, w_a2_f, b_a_f, w_a2_b, b_a_b)
        n1 = norm1[l][None, :]
        n2 = norm2[l][None, :]

        pl_ = _inproj(h, sc1, sh1, n1, wts, rope, tm)
        pc_ = _inproj(hc, csc1, csh1, n1, wts, None, lc)

        tab = _na_table(rpb_a[l], rows)
        o_a = _na_attention(pl_["qa"], pl_["ka"], pl_["vat"], pc_["ka"], pc_["vat"], tab)
        lamv = jnp.stack([lam_q1[l], lam_k1[l], lam_q2[l], lam_k2[l]])
        diff = (lamv, subln_b[l][:, None])
        o_b = _flash(pl_["qb"], pl_["kb"], pl_["vbt"], (pc_["kb"], pc_["vbt"]), diff, 256, 512, lam_init)

        s0 = jnp.zeros((bsz, 2, QK_W, GLA_DV), F32)
        ocf_c, ocb_c, s_ctx = _gla(pc_, s0, lc)
        ocf, ocb, _ = _gla(pl_, s_ctx, tb)

        w_o = w_out[l].astype(BF16)
        on = onorm_c[l][None, :]
        wg, wu, wd = w_g[l].astype(BF16), w_u[l].astype(BF16), w_d[l].astype(BF16)
        cw, cb = conv_w[l], conv_b[l][None, :]
        h = _mix_ffn(h, (o_a, o_b, ocf, ocb, pl_["gc"]), on, w_o, g1, sc2, sh2, n2, wg, wu, wd, cw, cb, g2, tm)
        if with_ctx_out:
            o_a_c = _flash(pc_["qa"], pc_["ka"], pc_["vat"], None, None, lc, lc)
            o_b_c = _flash(pc_["qb"], pc_["kb"], pc_["vbt"], None, diff, lc, lc, lam_init)
            hc = _mix_ffn(hc, (o_a_c, o_b_c, ocf_c, ocb_c, pc_["gc"]), on, w_o, cg1, csc2, csh2, n2,
                          wg, wu, wd, cw, cb, cg2, lc)
    return h
```

```python
import functools
import math

import jax
import jax.numpy as jnp
import numpy as np
from jax import lax
from jax.experimental import pallas as pl
from jax.experimental.pallas import tpu as pltpu

F32 = jnp.float32
BF16 = jnp.bfloat16

D_MODEL = 1024
GRID_W = 64
HEAD_W = 64
N_HEADS = 4
NA_WIN_ROWS = 8
NA_WIN_COLS = 16
NA_Q_ROWS = 4
NA_K_ROWS = 12
DA_QK_DIM = 32
GLA_DK = 64
GLA_DV = 128
GLA_CHUNK = 64
GLA_INTRA = 128
GLA_GATE_RANK = 16
GLA_GATE_NORM = 16.0
GATE_PAD = 128
FFN_DIM = 2816
ROPE_THETA = 10000.0
EPS = 1e-6
NEG_INF = -1e30
LOG2E = math.log2(math.e)
QK_W = N_HEADS * HEAD_W
VT_ROWS = HEAD_W + 16
VT_W = N_HEADS * VT_ROWS
GLA_V_W = N_HEADS * GLA_DV
VMEM_LIMIT = 56 * 1024 * 1024
TOKEN_TILE = 512
FLASH_TQ = 256
FLASH_TK = 512

_NT = (((1,), (1,)), ((), ()))


def _cparams(n_axes):
    return pltpu.CompilerParams(dimension_semantics=("arbitrary",) * n_axes,
                                vmem_limit_bytes=VMEM_LIMIT)


def _split_bf16(x, parts):
    out = []
    r = x
    for _ in range(parts):
        p = r.astype(BF16)
        out.append(p)
        r = r - p.astype(F32)
    return out


def _dot_f32_lhs(x, m_bf16, parts):
    acc = None
    for p in _split_bf16(x, parts):
        t = jnp.dot(p, m_bf16, preferred_element_type=F32)
        acc = t if acc is None else acc + t
    return acc


def _dot_f32_rhs(m_bf16, x, parts):
    acc = None
    for p in _split_bf16(x, parts):
        t = jnp.dot(m_bf16, p, preferred_element_type=F32)
        acc = t if acc is None else acc + t
    return acc


def _silu(x):
    return x * (1.0 / (1.0 + jnp.exp(-x)))


def _log_sigmoid(x):
    return jnp.minimum(x, 0.0) - jnp.log(1.0 + jnp.exp(-jnp.abs(x)))


def _ada_kernel(c_ref, w_ref, b_ref, o_ref):
    s = _silu(c_ref[...])
    w = w_ref[0]
    acc = None
    for sp in _split_bf16(s, 3):
        for wp in _split_bf16(w, 2):
            t = jnp.dot(sp, wp, preferred_element_type=F32)
            acc = t if acc is None else acc + t
    o_ref[0] = acc + b_ref[0]


def _ada(cvec, w_ada, b_ada):
    n_l, d, n6 = w_ada.shape
    tn = 1536
    return pl.pallas_call(
        _ada_kernel,
        grid=(n_l, n6 // tn),
        in_specs=[pl.BlockSpec((16, d), lambda l, j: (0, 0)),
                  pl.BlockSpec((1, d, tn), lambda l, j: (l, 0, j)),
                  pl.BlockSpec((1, 1, tn), lambda l, j: (l, 0, j))],
        out_specs=pl.BlockSpec((1, 16, tn), lambda l, j: (l, 0, j)),
        out_shape=jax.ShapeDtypeStruct((n_l, 16, n6), F32),
        compiler_params=_cparams(2),
        name="ada_proj",
    )(cvec, w_ada, b_ada.reshape(n_l, 1, n6))


def _group_rms(x, gmat, gain, gsz):
    ssq = _dot_f32_lhs(x * x, gmat, 1)
    return x * lax.rsqrt(ssq * (1.0 / gsz) + EPS) * gain


def _inproj_kernel(h_ref, sc_ref, sh_ref, n1_ref, wn_ref, wt_ref, g64_ref, g32_ref, gains_ref,
                   w2_ref, b2_ref, *rest, rope):
    if rope:
        cos_ref, s1_ref, s2_ref = rest[:3]
        rest = rest[3:]
    (qa_ref, ka_ref, qb_ref, kb_ref, qc_ref, kc_ref, vc_ref, gc_ref, gf_ref, gb_ref,
     vat_ref, vbt_ref, y_scr) = rest

    x = h_ref[0]
    y = x * lax.rsqrt(jnp.mean(x * x, axis=-1, keepdims=True) + EPS) * n1_ref[...]
    xn = (y * (1.0 + sc_ref[0]) + sh_ref[0]).astype(BF16)

    y_scr[...] = jnp.dot(xn, wn_ref[...], preferred_element_type=F32)
    vt = lax.dot_general(wt_ref[...], xn, _NT, preferred_element_type=F32)

    def proj(lo, hi):
        return y_scr[:, lo:hi]

    g64 = g64_ref[...]
    g32 = g32_ref[...]
    gains = gains_ref[...]

    def rot(v):
        if not rope:
            return v
        return (v * cos_ref[...] + pltpu.roll(v, QK_W - 8, 1) * s1_ref[...]
                + pltpu.roll(v, 8, 1) * s2_ref[...])

    qa_ref[0] = (_group_rms(proj(0, 256), g64, gains[0:1], HEAD_W) * (HEAD_W ** -0.5 * LOG2E)).astype(BF16)
    ka_ref[0] = _group_rms(proj(256, 512), g64, gains[1:2], HEAD_W).astype(BF16)
    qb_ref[0] = (rot(_group_rms(proj(512, 768), g32, gains[2:3], DA_QK_DIM))
                 * (DA_QK_DIM ** -0.5 * LOG2E)).astype(BF16)
    kb_ref[0] = rot(_group_rms(proj(768, 1024), g32, gains[3:4], DA_QK_DIM)).astype(BF16)
    qc_ref[0] = proj(1024, 1280) * (GLA_DK ** -0.5)
    kc_ref[0] = proj(1280, 1536)
    vc_ref[0] = proj(1536, 2048).astype(BF16)
    gc_ref[0] = proj(2048, 2560)
    a_lr = proj(2560, 2560 + GATE_PAD)
    a_hi, a_lo = _split_bf16(a_lr, 2)
    pre = (jnp.dot(a_hi, w2_ref[0], preferred_element_type=F32)
           + jnp.dot(a_lo, w2_ref[0], preferred_element_type=F32)
           + jnp.dot(a_hi, w2_ref[1], preferred_element_type=F32))
    gate = _log_sigmoid(pre + b2_ref[...]) * (1.0 / GLA_GATE_NORM)
    gf_ref[0] = gate[:, :QK_W]
    gb_ref[0] = gate[:, QK_W:]
    ones = jnp.ones((VT_ROWS - HEAD_W, vt.shape[1]), BF16)
    for grp, ref in enumerate((vat_ref, vbt_ref)):
        for h in range(N_HEADS):
            src = (grp * N_HEADS + h) * HEAD_W
            ref[0, h * VT_ROWS:h * VT_ROWS + HEAD_W, :] = vt[src:src + HEAD_W].astype(BF16)
            ref[0, h * VT_ROWS + HEAD_W:(h + 1) * VT_ROWS, :] = ones


def _inproj(h, sc, sh, n1, wts, rope_tabs, tm):
    b, t, d = h.shape
    nt = t // tm
    rope = rope_tabs is not None
    const = lambda shape: pl.BlockSpec(shape, lambda i, bb: (0,) * len(shape))
    tok = lambda w: pl.BlockSpec((1, tm, w), lambda i, bb: (bb, i, 0))
    mod = pl.BlockSpec((1, 1, d), lambda i, bb: (bb, 0, 0))
    in_specs = [tok(d), mod, mod, const((1, d)), const(wts["wn"].shape), const(wts["wt"].shape),
                const((QK_W, QK_W)), const((QK_W, QK_W)), const((4, QK_W)),
                const((2, GATE_PAD, 2 * QK_W)), const((1, 2 * QK_W))]
    args = [h, sc, sh, n1, wts["wn"], wts["wt"], wts["g64"], wts["g32"], wts["gains"],
            wts["w2"], wts["b2"]]
    if rope:
        in_specs += [pl.BlockSpec((tm, QK_W), lambda i, bb: (i, 0))] * 3
        args += list(rope_tabs)
    tspec = pl.BlockSpec((1, VT_W, tm), lambda i, bb: (bb, 0, i))
    out_specs = [tok(QK_W)] * 6 + [tok(GLA_V_W), tok(GLA_V_W), tok(QK_W), tok(QK_W), tspec, tspec]
    sd = jax.ShapeDtypeStruct
    out_shape = [sd((b, t, QK_W), BF16)] * 4 + [sd((b, t, QK_W), F32)] * 2 + [
        sd((b, t, GLA_V_W), BF16), sd((b, t, GLA_V_W), F32), sd((b, t, QK_W), F32),
        sd((b, t, QK_W), F32), sd((b, VT_W, t), BF16), sd((b, VT_W, t), BF16)]
    outs = pl.pallas_call(
        functools.partial(_inproj_kernel, rope=rope),
        grid=(nt, b), in_specs=in_specs, out_specs=out_specs, out_shape=out_shape,
        scratch_shapes=[pltpu.VMEM((tm, wts["wn"].shape[1]), F32)],
        compiler_params=_cparams(2), name="in_proj_rope" if rope else "in_proj",
    )(*args)
    names = ("qa", "ka", "qb", "kb", "qc", "kc", "vc", "gc", "gf", "gb", "vat", "vbt")
    return dict(zip(names, outs))


def _head_mask(shape, lo, hi):
    lane = lax.broadcasted_iota(jnp.int32, shape, 1)
    return (lane >= lo) & (lane < hi)


def _na_kernel(q_ref, k_ref, vt_ref, kc_ref, vct_ref, tab_ref, o_ref, qm_scr, s_scr, *, n_rows):
    i = pl.program_id(1)
    nq = NA_Q_ROWS * GRID_W
    nk = NA_K_ROWS * GRID_W
    base_row = jnp.clip(NA_Q_ROWS * i - NA_WIN_ROWS // 2, 0, n_rows - NA_K_ROWS)
    base = pl.multiple_of(base_row * GRID_W, 256)
    btype = jnp.where(i == 0, 0, jnp.where(i == pl.num_programs(1) - 1, 2, 1))
    kw = k_ref[0, pl.ds(base, nk), :]
    vw = vt_ref[0, :, pl.ds(base, nk)]
    kc = kc_ref[0]
    vct = vct_ref[0]
    q = q_ref[0]
    zero = jnp.zeros_like(q)
    for h in range(N_HEADS):
        qm_scr[h * nq:(h + 1) * nq, :] = jnp.where(_head_mask(q.shape, h * HEAD_W, (h + 1) * HEAD_W), q, zero)
    s_scr[...] = lax.dot_general(jnp.concatenate([kw, kc], axis=0), qm_scr[...], _NT,
                                 preferred_element_type=F32)
    vt_all = jnp.concatenate([vw, vct], axis=1)
    for h in range(N_HEADS):
        cols = slice(h * nq, (h + 1) * nq)
        sw = s_scr[0:nk, cols] + tab_ref[btype, h]
        sc = s_scr[nk:, cols]
        m = jnp.maximum(jnp.max(sw, axis=0, keepdims=True), jnp.max(sc, axis=0, keepdims=True))
        p = jnp.concatenate([jnp.exp2(sw - m), jnp.exp2(sc - m)], axis=0).astype(BF16)
        ot = jnp.dot(vt_all[h * VT_ROWS:(h + 1) * VT_ROWS], p, preferred_element_type=F32)
        o_ref[0, :, h * HEAD_W:(h + 1) * HEAD_W] = (ot[:HEAD_W] / ot[HEAD_W:HEAD_W + 1]).T.astype(BF16)


RPB_ROWS = 2 * NA_WIN_ROWS - 1
RPB_COLS = 2 * NA_WIN_COLS - 1


def _na_table_kernel(rpb_ref, o_ref, *, n_rows):
    h = pl.program_id(0)
    c = lax.broadcasted_iota(jnp.int32, (GRID_W, GRID_W), 0)
    w = lax.broadcasted_iota(jnp.int32, (GRID_W, GRID_W), 1)
    co = jnp.clip(c - w, -(NA_WIN_COLS - 1), NA_WIN_COLS - 1) + NA_WIN_COLS - 1
    c0 = jnp.clip(w - NA_WIN_COLS // 2, 0, GRID_W - NA_WIN_COLS)
    col_ok = (c >= c0) & (c < c0 + NA_WIN_COLS)
    tiles = []
    for ro in range(RPB_ROWS):
        t = jnp.zeros((GRID_W, GRID_W), F32)
        for kk in range(RPB_COLS):
            t = jnp.where(co == kk, rpb_ref[(h * RPB_ROWS + ro) * RPB_COLS + kk], t)
        tiles.append(jnp.where(col_ok, t * LOG2E, NEG_INF))
    outside = jnp.full((GRID_W, GRID_W), NEG_INF, F32)
    n_blocks = n_rows // NA_Q_ROWS
    for bt, blk in enumerate((0, 1, n_blocks - 1)):
        base_row = min(max(NA_Q_ROWS * blk - NA_WIN_ROWS // 2, 0), n_rows - NA_K_ROWS)
        for a in range(NA_K_ROWS):
            for b in range(NA_Q_ROWS):
                key_row, q_row = base_row + a, NA_Q_ROWS * blk + b
                r0 = min(max(q_row - NA_WIN_ROWS // 2, 0), n_rows - NA_WIN_ROWS)
                inside = r0 <= key_row < r0 + NA_WIN_ROWS
                o_ref[bt, 0, a * GRID_W:(a + 1) * GRID_W, b * GRID_W:(b + 1) * GRID_W] = (
                    tiles[key_row - q_row + NA_WIN_ROWS - 1] if inside else outside)


def _na_table(rpb, n_rows):
    n_h = rpb.shape[0]
    nk, nq = NA_K_ROWS * GRID_W, NA_Q_ROWS * GRID_W
    flat = jnp.zeros((2048,), F32).at[:n_h * RPB_ROWS * RPB_COLS].set(rpb.reshape(-1))
    return pl.pallas_call(
        functools.partial(_na_table_kernel, n_rows=n_rows), grid=(n_h,),
        in_specs=[pl.BlockSpec(memory_space=pltpu.SMEM)],
        out_specs=pl.BlockSpec((3, 1, nk, nq), lambda h: (0, h, 0, 0)),
        out_shape=jax.ShapeDtypeStruct((3, n_h, nk, nq), F32),
        compiler_params=_cparams(1), name="na_bias_table",
    )(flat)


def _na_attention(q, k, vt, kc, vct, tab):
    b, s, _ = q.shape
    rows = s // GRID_W
    nb = rows // NA_Q_ROWS
    nq = NA_Q_ROWS * GRID_W
    lc = kc.shape[1]
    return pl.pallas_call(
        functools.partial(_na_kernel, n_rows=rows),
        grid=(b, nb),
        in_specs=[pl.BlockSpec((1, nq, QK_W), lambda bb, i: (bb, i, 0)),
                  pl.BlockSpec((1, s, QK_W), lambda bb, i: (bb, 0, 0)),
                  pl.BlockSpec((1, VT_W, s), lambda bb, i: (bb, 0, 0)),
                  pl.BlockSpec((1, lc, QK_W), lambda bb, i: (bb, 0, 0)),
                  pl.BlockSpec((1, VT_W, lc), lambda bb, i: (bb, 0, 0)),
                  pl.BlockSpec(tab.shape, lambda bb, i: (0, 0, 0, 0), pipeline_mode=pl.Buffered(1))],
        out_specs=pl.BlockSpec((1, nq, QK_W), lambda bb, i: (bb, i, 0)),
        out_shape=jax.ShapeDtypeStruct((b, s, QK_W), BF16),
        scratch_shapes=[pltpu.VMEM((N_HEADS * nq, QK_W), BF16),
                        pltpu.VMEM((NA_K_ROWS * GRID_W + lc, N_HEADS * nq), F32)],
        compiler_params=_cparams(2), name="na_attention",
    )(q, k, vt, kc, vct, tab)


def _flash_kernel(*refs, n_comp, tk, n_chunks, has_extra, lam_init):
    q_ref, k_ref, vt_ref = refs[:3]
    refs = refs[3:]
    if has_extra:
        kx_ref, vxt_ref = refs[:2]
        refs = refs[2:]
    if n_comp == 2:
        lam_ref, gain_ref = refs[:2]
        refs = refs[2:]
    o_ref, qm_scr, m_scr, acc_scr, s0_scr, s1_scr, c0_scr, c1_scr = refs

    q = q_ref[0]
    tq = q.shape[0]
    dsub = HEAD_W // n_comp
    ng = N_HEADS * n_comp
    zero = jnp.zeros_like(q)
    for g in range(ng):
        qm_scr[g * tq:(g + 1) * tq, :] = jnp.where(_head_mask(q.shape, g * dsub, (g + 1) * dsub), q, zero)
    m_scr[...] = jnp.full(m_scr.shape, NEG_INF, F32)
    acc_scr[...] = jnp.zeros(acc_scr.shape, F32)

    def issue_scores(dst, cmax, kc):
        nkeys = kc.shape[0]

        def issue(g):
            sg = lax.dot_general(kc, qm_scr[g * tq:(g + 1) * tq, :], _NT, preferred_element_type=F32)
            dst[0:nkeys, g * tq:(g + 1) * tq] = sg
            cmax[g:g + 1] = jnp.max(sg, axis=0, keepdims=True)
        return issue

    def softmax_pv(src, cmax, vtc, before_group=None):
        nkeys = vtc.shape[1]
        for g in range(ng):
            if before_group is not None:
                before_group(g)
            h = g // n_comp
            m_old = m_scr[g:g + 1]
            m_new = jnp.maximum(m_old, cmax[g:g + 1])
            alpha = jnp.exp2(m_old - m_new)
            p = jnp.exp2(src[0:nkeys, g * tq:(g + 1) * tq] - m_new)
            rows = slice(g * VT_ROWS, (g + 1) * VT_ROWS)
            acc_scr[rows] = alpha * acc_scr[rows] + jnp.dot(
                vtc[h * VT_ROWS:(h + 1) * VT_ROWS], p.astype(BF16), preferred_element_type=F32)
            m_scr[g:g + 1] = m_new

    def keys(j):
        return k_ref[0, pl.ds(pl.multiple_of(j * tk, tk), tk), :]

    def values_t(j):
        return vt_ref[0, :, pl.ds(pl.multiple_of(j * tk, tk), tk)]

    first = issue_scores(s0_scr, c0_scr, keys(0))
    for g in range(ng):
        first(g)
    if n_chunks > 1:
        def pair(j, last):
            softmax_pv(s0_scr, c0_scr, values_t(j), issue_scores(s1_scr, c1_scr, keys(j + 1)))
            if not last:
                nxt = issue_scores(s0_scr, c0_scr, keys(j + 2))
            elif has_extra:
                nxt = issue_scores(s0_scr, c0_scr, kx_ref[0])
            else:
                nxt = None
            softmax_pv(s1_scr, c1_scr, values_t(j + 1), nxt)

        def body(i, carry):
            pair(2 * i, False)
            return carry

        lax.fori_loop(0, n_chunks // 2 - 1, body, 0)
        pair(n_chunks - 2, True)
        if has_extra:
            softmax_pv(s0_scr, c0_scr, vxt_ref[0])
    else:
        nxt = issue_scores(s1_scr, c1_scr, kx_ref[0]) if has_extra else None
        softmax_pv(s0_scr, c0_scr, values_t(0), nxt)
        if has_extra:
            softmax_pv(s1_scr, c1_scr, vxt_ref[0])

    if n_comp == 2:
        lv = lam_ref[...]
        lam = (jnp.exp(jnp.sum(lv[0:1] * lv[1:2], axis=1, keepdims=True))
               - jnp.exp(jnp.sum(lv[2:3] * lv[3:4], axis=1, keepdims=True)) + lam_init)
    for h in range(N_HEADS):
        cols = slice(h * HEAD_W, (h + 1) * HEAD_W)
        if n_comp == 2:
            a0 = acc_scr[2 * h * VT_ROWS:(2 * h + 1) * VT_ROWS]
            a1 = acc_scr[(2 * h + 1) * VT_ROWS:(2 * h + 2) * VT_ROWS]
            o = (a0[:HEAD_W] / a0[HEAD_W:HEAD_W + 1]
                 - lam * (a1[:HEAD_W] / a1[HEAD_W:HEAD_W + 1]))
            o = (o * lax.rsqrt(jnp.mean(o * o, axis=0, keepdims=True) + EPS)
                 * gain_ref[...] * (1.0 - lam_init))
        else:
            a0 = acc_scr[h * VT_ROWS:(h + 1) * VT_ROWS]
            o = a0[:HEAD_W] / a0[HEAD_W:HEAD_W + 1]
        o_ref[0, :, cols] = o.T.astype(BF16)


def _flash(q, k, vt, extra, diff, tq, tk, lam_init=0.0):
    b, t, _ = q.shape
    tkeys = k.shape[1]
    n_comp = 2 if diff is not None else 1
    ng = N_HEADS * n_comp
    in_specs = [pl.BlockSpec((1, tq, QK_W), lambda bb, i: (bb, i, 0)),
                pl.BlockSpec((1, tkeys, QK_W), lambda bb, i: (bb, 0, 0)),
                pl.BlockSpec((1, VT_W, tkeys), lambda bb, i: (bb, 0, 0))]
    args = [q, k, vt]
    if extra is not None:
        lx = extra[0].shape[1]
        in_specs += [pl.BlockSpec((1, lx, QK_W), lambda bb, i: (bb, 0, 0)),
                     pl.BlockSpec((1, VT_W, lx), lambda bb, i: (bb, 0, 0))]
        args += list(extra)
    if diff is not None:
        in_specs += [pl.BlockSpec((4, DA_QK_DIM), lambda bb, i: (0, 0)),
                     pl.BlockSpec((HEAD_W, 1), lambda bb, i: (0, 0))]
        args += list(diff)
    return pl.pallas_call(
        functools.partial(_flash_kernel, n_comp=n_comp, tk=tk, n_chunks=tkeys // tk,
                          has_extra=extra is not None, lam_init=lam_init),
        grid=(b, t // tq), in_specs=in_specs,
        out_specs=pl.BlockSpec((1, tq, QK_W), lambda bb, i: (bb, i, 0)),
        out_shape=jax.ShapeDtypeStruct((b, t, QK_W), BF16),
        scratch_shapes=[pltpu.VMEM((ng * tq, QK_W), BF16), pltpu.VMEM((ng, tq), F32),
                        pltpu.VMEM((ng * VT_ROWS, tq), F32),
                        pltpu.VMEM((tk, ng * tq), F32), pltpu.VMEM((tk, ng * tq), F32),
                        pltpu.VMEM((ng, tq), F32), pltpu.VMEM((ng, tq), F32)],
        compiler_params=_cparams(2),
        name="diff_attention" if diff is not None else "dense_attention",
    )(*args)


def _gla_dir(q_ref, k_ref, v_ref, g_ref, o_ref, state, cum_ref, attn_scr, o_scr, upd_scr, sin_scr, n_chunks, reverse):
    c = GLA_CHUNK
    tb = n_chunks * c
    b_all = _dot_f32_rhs(cum_ref[...], g_ref[0], 2)
    b3 = b_all.reshape(n_chunks, c, QK_W)
    btot3 = b3[:, 0:1] if reverse else b3[:, c - 1:c]
    bmid3 = b3[:, c // 2:c // 2 + 1]
    q3 = q_ref[0].reshape(n_chunks, c, QK_W)
    k3 = k_ref[0].reshape(n_chunks, c, QK_W)
    q_in = (q3 * jnp.exp(b3 - bmid3)).reshape(tb, QK_W).astype(BF16)
    k_in = (k3 * jnp.exp(bmid3 - b3)).reshape(tb, QK_W).astype(BF16)
    q_st = (q3 * jnp.exp(b3)).reshape(tb, QK_W).astype(BF16)
    k_st = (k3 * jnp.exp(btot3 - b3)).reshape(tb, QK_W)
    dec_all = jnp.exp(btot3)
    v = v_ref[0]

    nb_i = min(GLA_INTRA, tb)
    t = lax.broadcasted_iota(jnp.int32, (nb_i, nb_i), 0)
    s = lax.broadcasted_iota(jnp.int32, (nb_i, nb_i), 1)
    keep = ((t // c) == (s // c)) & ((s >= t) if reverse else (s <= t))
    zero = jnp.zeros((nb_i, QK_W), BF16)
    hm_i = [_head_mask((nb_i, QK_W), h * GLA_DK, (h + 1) * GLA_DK) for h in range(N_HEADS)]
    blocks = [(r0, h) for r0 in range(0, tb, nb_i) for h in range(N_HEADS)]
    for n, (r0, h) in enumerate(blocks):
        attn = lax.dot_general(q_in[r0:r0 + nb_i], jnp.where(hm_i[h], k_in[r0:r0 + nb_i], zero), _NT,
                               preferred_element_type=F32)
        attn_scr[n] = jnp.where(keep, attn, 0.0).astype(BF16)
    for n, (r0, h) in enumerate(blocks):
        vr = slice(h * GLA_DV, (h + 1) * GLA_DV)
        o_scr[r0:r0 + nb_i, vr] = jnp.dot(attn_scr[n], v[r0:r0 + nb_i, vr], preferred_element_type=F32)

    order = list(range(n_chunks - 1, -1, -1) if reverse else range(n_chunks))
    for ci in order:
        r0 = ci * c
        k_t = k_st[r0:r0 + c].T.astype(BF16)
        for h in range(N_HEADS):
            vr = slice(h * GLA_DV, (h + 1) * GLA_DV)
            upd_scr[ci, h * GLA_DK:(h + 1) * GLA_DK, :] = jnp.dot(
                k_t[h * GLA_DK:(h + 1) * GLA_DK], v[r0:r0 + c, vr], preferred_element_type=F32)
    s_cur = state[...]
    for ci in order:
        sin_scr[ci] = s_cur.astype(BF16)
        dec = jnp.broadcast_to(dec_all[ci], (GLA_DV, QK_W)).T
        s_cur = dec * s_cur + upd_scr[ci]
    state[...] = s_cur
    qzero = jnp.zeros((c, QK_W), BF16)
    hmasks = [_head_mask((c, QK_W), h * GLA_DK, (h + 1) * GLA_DK) for h in range(N_HEADS)]
    for ci in order:
        r0 = ci * c
        qc = q_st[r0:r0 + c]
        q_stack = jnp.concatenate([jnp.where(hm, qc, qzero) for hm in hmasks], axis=0)
        o_st = jnp.dot(q_stack, sin_scr[ci], preferred_element_type=F32)
        for h in range(N_HEADS):
            vr = slice(h * GLA_DV, (h + 1) * GLA_DV)
            o_ref[0, r0:r0 + c, vr] = (o_scr[r0:r0 + c, vr] + o_st[h * c:(h + 1) * c]).astype(o_ref.dtype)


def _gla_kernel(qf_ref, kf_ref, vf_ref, gf_ref, qb_ref, kb_ref, vb_ref, gb_ref, s0_ref,
                cumf_ref, cumb_ref, of_ref, ob_ref, sfin_ref, sf, sb, attn_f, attn_b, o_f, o_b,
                upd_f, upd_b, sin_f, sin_b, *, n_chunks):
    i = pl.program_id(1)

    @pl.when(i == 0)
    def _():
        sf[...] = s0_ref[0, 0]
        sb[...] = s0_ref[0, 1]

    _gla_dir(qf_ref, kf_ref, vf_ref, gf_ref, of_ref, sf, cumf_ref, attn_f, o_f, upd_f, sin_f, n_chunks, False)
    _gla_dir(qb_ref, kb_ref, vb_ref, gb_ref, ob_ref, sb, cumb_ref, attn_b, o_b, upd_b, sin_b, n_chunks, True)

    @pl.when(i == pl.num_programs(1) - 1)
    def _():
        sfin_ref[0, 0] = sf[...]
        sfin_ref[0, 1] = sb[...]


def _gla(p, s0, tb):
    b, t, _ = p["qc"].shape
    nb = t // tb
    n_chunks = tb // GLA_CHUNK
    nb_i = min(GLA_INTRA, tb)
    n_blk = (tb // nb_i) * N_HEADS
    idx = np.arange(tb)
    same = (idx[:, None] // GLA_CHUNK) == (idx[None, :] // GLA_CHUNK)
    cumf = jnp.asarray(same & (idx[None, :] <= idx[:, None]), BF16)
    cumb = jnp.asarray(same & (idx[None, :] >= idx[:, None]), BF16)
    fwd = lambda w: pl.BlockSpec((1, tb, w), lambda bb, i: (bb, i, 0))
    bwd = lambda w: pl.BlockSpec((1, tb, w), lambda bb, i: (bb, nb - 1 - i, 0))
    st = pl.BlockSpec((1, 2, QK_W, GLA_DV), lambda bb, i: (bb, 0, 0, 0))
    cm = pl.BlockSpec((tb, tb), lambda bb, i: (0, 0))
    sd = jax.ShapeDtypeStruct
    return pl.pallas_call(
        functools.partial(_gla_kernel, n_chunks=n_chunks),
        grid=(b, nb),
        in_specs=[fwd(QK_W), fwd(QK_W), fwd(GLA_V_W), fwd(QK_W),
                  bwd(QK_W), bwd(QK_W), bwd(GLA_V_W), bwd(QK_W), st, cm, cm],
        out_specs=[fwd(GLA_V_W), bwd(GLA_V_W), st],
        out_shape=[sd((b, t, GLA_V_W), BF16), sd((b, t, GLA_V_W), BF16), sd((b, 2, QK_W, GLA_DV), F32)],
        scratch_shapes=[pltpu.VMEM((QK_W, GLA_DV), F32), pltpu.VMEM((QK_W, GLA_DV), F32)]
        + [pltpu.VMEM((n_blk, nb_i, nb_i), BF16)] * 2 + [pltpu.VMEM((tb, GLA_V_W), F32)] * 2
        + [pltpu.VMEM((n_chunks, QK_W, GLA_DV), F32)] * 2 + [pltpu.VMEM((n_chunks, QK_W, GLA_DV), BF16)] * 2,
        compiler_params=_cparams(2), name="gla_scan",
    )(p["qc"], p["kc"], p["vc"], p["gf"], p["qc"], p["kc"], p["vc"], p["gb"], s0, cumf, cumb)


FFN_HALO = 16
FFN_CHUNKS = ((0, 1280), (1280, 1536))
N_MIX = 5


def _mix_rows(oa, ob, ocf, ocb, gc, onorm):
    oc = ocf.astype(F32) + ocb.astype(F32)
    parts = [oa, ob]
    for h in range(N_HEADS):
        vr = slice(h * GLA_DV, (h + 1) * GLA_DV)
        x = oc[:, vr]
        x = x * lax.rsqrt(jnp.mean(x * x, axis=-1, keepdims=True) + EPS) * onorm
        parts.append((x * _silu(gc[:, vr])).astype(BF16))
    return jnp.concatenate(parts, axis=1)


def _ffn_kernel(*refs, chunks):
    h_ref, hp_ref, hn_ref = refs[0:3]
    mix_main, mix_prev, mix_next = (refs[3 + N_MIX * j:3 + N_MIX * (j + 1)] for j in range(3))
    (on_ref, wo_ref, g1_ref, sc_ref, sh_ref, n2_ref, wg_ref, wu_ref, wd_ref, cw_ref, cb_ref, g2_ref,
     o_ref, mix_scr, xn_scr, a_scr) = refs[3 + 3 * N_MIX:]
    i = pl.program_id(1)
    nt = pl.num_programs(1)
    tm = h_ref.shape[1]
    lo, hi = FFN_HALO, FFN_HALO + tm
    n2 = n2_ref[...]
    sc = sc_ref[0]
    sh = sh_ref[0]
    g1 = g1_ref[0]
    onorm = on_ref[...]

    mix_scr[0:lo] = _mix_rows(*(r[0] for r in mix_prev), onorm)
    mix_scr[lo:hi] = _mix_rows(*(r[0] for r in mix_main), onorm)
    mix_scr[hi:] = _mix_rows(*(r[0] for r in mix_next), onorm)
    y = jnp.dot(mix_scr[...], wo_ref[...], preferred_element_type=F32)

    def normed(x):
        z = x * lax.rsqrt(jnp.mean(x * x, axis=-1, keepdims=True) + EPS) * n2
        return z * (1.0 + sc) + sh

    x = h_ref[0] + g1 * y[lo:hi]
    prev_ok = (i > 0).astype(F32)
    next_ok = (i < nt - 1).astype(F32)
    xn_scr[0:lo] = (normed(hp_ref[0] + g1 * y[0:lo]) * prev_ok).astype(BF16)
    xn_scr[lo:hi] = normed(x).astype(BF16)
    xn_scr[hi:] = (normed(hn_ref[0] + g1 * y[hi:]) * next_ok).astype(BF16)

    acc = None
    for f0, fc in chunks:
        a_scr[:, :fc] = jnp.dot(xn_scr[...], wg_ref[:, f0:f0 + fc], preferred_element_type=F32)
        u = jnp.dot(xn_scr[lo:hi], wu_ref[:, f0:f0 + fc], preferred_element_type=F32)
        cw = cw_ref[:, f0:f0 + fc]
        a = (cb_ref[:, f0:f0 + fc]
             + a_scr[lo - 1:hi - 1, :fc] * cw[0:1]
             + a_scr[lo:hi, :fc] * cw[1:2]
             + a_scr[lo + 1:hi + 1, :fc] * cw[2:3])
        g = (_silu(a) * u).astype(BF16)
        t = jnp.dot(g, wd_ref[f0:f0 + fc], preferred_element_type=F32)
        acc = t if acc is None else acc + t
    o_ref[0] = x + g2_ref[0] * acc


def _mix_ffn(h, mix, onorm, w_out, g1, sc, sh, n2, wg, wu, wd, cw, cb, g2, tm):
    b, t, d = h.shape
    nt = t // tm
    hb = tm // FFN_HALO
    tok = lambda w: pl.BlockSpec((1, tm, w), lambda bb, i: (bb, i, 0))
    prev = lambda w: pl.BlockSpec((1, FFN_HALO, w), lambda bb, i: (bb, jnp.maximum(i * hb - 1, 0), 0))
    nxt = lambda w: pl.BlockSpec((1, FFN_HALO, w), lambda bb, i: (bb, jnp.minimum((i + 1) * hb, nt * hb - 1), 0))
    mod = pl.BlockSpec((1, 1, d), lambda bb, i: (bb, 0, 0))
    const = lambda shape: pl.BlockSpec(shape, lambda bb, i: (0,) * len(shape),
                                       pipeline_mode=pl.Buffered(1))
    widths = [m.shape[-1] for m in mix]
    in_specs = ([tok(d), prev(d), nxt(d)] + [tok(w) for w in widths] + [prev(w) for w in widths]
                + [nxt(w) for w in widths]
                + [const((1, GLA_DV)), const(w_out.shape), mod, mod, mod, const((1, d)), const(wg.shape),
                   const(wu.shape), const(wd.shape), const(cw.shape), const(cb.shape), mod])
    return pl.pallas_call(
        functools.partial(_ffn_kernel, chunks=FFN_CHUNKS), grid=(b, nt),
        in_specs=in_specs,
        out_specs=tok(d), out_shape=jax.ShapeDtypeStruct((b, t, d), F32),
        scratch_shapes=[pltpu.VMEM((tm + 2 * FFN_HALO, d), BF16),
                        pltpu.VMEM((tm + 2 * FFN_HALO, d), BF16),
                        pltpu.VMEM((tm + 2 * FFN_HALO, max(fc for _, fc in FFN_CHUNKS)), F32)],
        compiler_params=_cparams(2), name="mix_ffn",
    )(h, h, h, *mix, *mix, *mix, onorm, w_out, g1, sc, sh, n2, wg, wu, wd, cw, cb, g2)


def _rope_tables(s):
    t = np.arange(s)
    row, col = t // GRID_W, t % GRID_W
    lane = np.arange(QK_W)
    jj = lane % DA_QK_DIM
    nf = DA_QK_DIM // 4
    inv = np.float32(ROPE_THETA) ** (-np.arange(nf, dtype=np.float32) / np.float32(nf))
    pos = np.where((jj // (2 * nf) == 0)[None, :], row[:, None], col[:, None]).astype(np.float32)
    ang = pos * inv[jj % nf][None, :]
    first = ((jj % (2 * nf)) < nf)[None, :]
    cos, sin = np.cos(ang.astype(np.float64)), np.sin(ang.astype(np.float64))
    as_f32 = lambda x: jnp.asarray(x.astype(np.float32))
    return as_f32(cos), as_f32(np.where(first, -sin, 0.0)), as_f32(np.where(first, 0.0, sin))


def _block_ones(gsz):
    g = np.arange(QK_W) // gsz
    return jnp.asarray(g[:, None] == g[None, :], BF16)


def _layer_weights(l, w_in, qn_a, kn_a, qn_b, kn_b, w_a2_f, b_a_f, w_a2_b, b_a_b):
    w = w_in[l]
    d = w.shape[0]
    wn = jnp.concatenate([w[:, 0:512], w[:, 768:1280], w[:, 1536:3104],
                          jnp.zeros((d, GATE_PAD - 2 * GLA_GATE_RANK), F32)], axis=1).astype(BF16)
    wt = jnp.concatenate([w[:, 512:768], w[:, 1280:1536]], axis=1).T.astype(BF16)
    gains = jnp.stack([jnp.tile(qn_a[l], N_HEADS), jnp.tile(kn_a[l], N_HEADS),
                       jnp.tile(qn_b[l], 2 * N_HEADS), jnp.tile(kn_b[l], 2 * N_HEADS)])
    w2 = jnp.zeros((GATE_PAD, 2 * QK_W), F32)
    w2 = w2.at[0:GLA_GATE_RANK, 0:QK_W].set(w_a2_f[l])
    w2 = w2.at[GLA_GATE_RANK:2 * GLA_GATE_RANK, QK_W:].set(w_a2_b[l])
    w2_hi = w2.astype(BF16)
    w2_lo = (w2 - w2_hi.astype(F32)).astype(BF16)
    b2 = jnp.concatenate([b_a_f[l], b_a_b[l]])[None, :]
    return {"wn": wn, "wt": wt, "g64": _block_ones(HEAD_W), "g32": _block_ones(DA_QK_DIM),
            "gains": gains, "w2": jnp.stack([w2_hi, w2_lo]), "b2": b2}


def kernel(x, c, ctx, c_ctx, norm1, norm2, w_ada, b_ada, w_in, qn_a, kn_a, rpb_a, qn_b, kn_b,
           lam_q1, lam_k1, lam_q2, lam_k2, subln_b, w_a2_f, b_a_f, w_a2_b, b_a_b, onorm_c, w_out,
           w_g, w_u, conv_w, conv_b, w_d):
    bsz, s, d = x.shape
    lc = ctx.shape[1]
    depth = w_in.shape[0]
    rows = s // GRID_W
    tm = min(TOKEN_TILE, s)
    tb = min(TOKEN_TILE, s)

    cvec = jnp.zeros((16, d), F32).at[:bsz].set(c).at[bsz].set(c_ctx)
    ada = _ada(cvec, w_ada, b_ada)
    rope = _rope_tables(s)

    h, hc = x, ctx
    for l in range(depth):
        with_ctx_out = l < depth - 1
        lam_init = 0.8 - 0.6 * math.exp(-0.3 * l)
        m = ada[l, :bsz].reshape(bsz, 1, 6, d)
        mc = jnp.broadcast_to(ada[l, bsz].reshape(1, 1, 6, d), (bsz, 1, 6, d))
        sh1, sc1, g1, sh2, sc2, g2 = (m[:, :, j] for j in range(6))
        csh1, csc1, cg1, csh2, csc2, cg2 = (mc[:, :, j] for j in range(6))
        wts = _layer_weights(l, w_in, qn_a, kn_a, qn_b, kn_b, w_a2_f, b_a_f, w_a2_b, b_a_b)
        n1 = norm1[l][None, :]
        n2 = norm2[l][None, :]

        pl_ = _inproj(h, sc1, sh1, n1, wts, rope, tm)
        pc_ = _inproj(hc, csc1, csh1, n1, wts, None, lc)

        tab = _na_table(rpb_a[l], rows)
        o_a = _na_attention(pl_["qa"], pl_["ka"], pl_["vat"], pc_["ka"], pc_["vat"], tab)
        lamv = jnp.stack([lam_q1[l], lam_k1[l], lam_q2[l], lam_k2[l]])
        diff = (lamv, subln_b[l][:, None])
        o_b = _flash(pl_["qb"], pl_["kb"], pl_["vbt"], (pc_["kb"], pc_["vbt"]), diff, FLASH_TQ, FLASH_TK, lam_init)

        s0 = jnp.zeros((bsz, 2, QK_W, GLA_DV), F32)
        ocf_c, ocb_c, s_ctx = _gla(pc_, s0, lc)
        ocf, ocb, _ = _gla(pl_, s_ctx, tb)

        w_o = w_out[l].astype(BF16)
        on = onorm_c[l][None, :]
        wg, wu, wd = w_g[l].astype(BF16), w_u[l].astype(BF16), w_d[l].astype(BF16)
        cw, cb = conv_w[l], conv_b[l][None, :]
        h = _mix_ffn(h, (o_a, o_b, ocf, ocb, pl_["gc"]), on, w_o, g1, sc2, sh2, n2, wg, wu, wd, cw, cb, g2, tm)
        if with_ctx_out:
            o_a_c = _flash(pc_["qa"], pc_["ka"], pc_["vat"], None, None, lc, lc)
            o_b_c = _flash(pc_["qb"], pc_["kb"], pc_["vbt"], None, diff, lc, lc, lam_init)
            hc = _mix_ffn(hc, (o_a_c, o_b_c, ocf_c, ocb_c, pc_["gc"]), on, w_o, cg1, csc2, csh2, n2,
                          wg, wu, wd, cw, cb, cg2, lc)
    return h
```
